```python
import math
import jax, jax.numpy as jnp
from jax import lax
import numpy as np

D_MODEL = 1024
BATCH = 8
SEQ = 2048
DEPTH = 4
DEC_BATCH = 128
DEC_SEQ = 4
PAST_LEN = 2048
PAGE_SIZE = 128

N_MIXERS = 2
CHUNK = 128
GM_WIDTH = D_MODEL
GM_GROUPS = 8
GM_GROUP_DIM = GM_WIDTH // GM_GROUPS
SB_HEADS = 16
SB_HEAD_DIM = D_MODEL // SB_HEADS
Q_BLOCK = 128
SB_BIAS_HI = -4.0
SB_BIAS_LO = -8.0
D_FF = 2816
N_EXPERTS = 8
TOP_K = 2
D_FF_EXPERT = 2816
LN_EPS = 1e-5
ALPHA = (2 * DEPTH) ** 0.25
BETA_INIT = (8 * DEPTH) ** -0.25
N_GM_LAYERS = (DEPTH + 1) // 2
N_SB_LAYERS = DEPTH // 2
N_DENSE_LAYERS = (DEPTH + 1) // 2
N_MOE_LAYERS = DEPTH // 2

kernel_name = "hybrid_gmlp_stickbreaking_decoder_step"


def layer_norm(x, g, b):
    xf = x.astype(jnp.float32)
    mu = jnp.mean(xf, axis=-1, keepdims=True)
    xc = xf - mu
    var = jnp.mean(xc * xc, axis=-1, keepdims=True)
    y = xc * lax.rsqrt(var + LN_EPS) * g.astype(jnp.float32) + b.astype(jnp.float32)
    return y.astype(x.dtype)


def gmlp_mixer(x, w_uv, vn_g, vn_b, w_s, b_s, w_o, chunk):
    n, t, _ = x.shape
    h = jax.nn.gelu(x @ w_uv, approximate=False)
    u, v = h[..., :GM_WIDTH], h[..., GM_WIDTH:]
    v = layer_norm(v, vn_g, vn_b)
    vc = v.reshape(n, t // chunk, chunk, GM_GROUPS, GM_GROUP_DIM)
    causal = jnp.tril(jnp.ones((chunk, chunk), dtype=bool))
    ws = jnp.where(causal, w_s[:, :chunk, :chunk], 0)
    s = jnp.einsum("gts,ncsgd->nctgd", ws, vc) + b_s[:, :chunk].T[None, None, :, :, None]
    y = u * s.reshape(n, t, GM_WIDTH)
    return y @ w_o, v


def qkv_heads(x, w_qkv):
    n, t, _ = x.shape
    qkv = (x @ w_qkv).reshape(n, t, 3, SB_HEADS, SB_HEAD_DIM)
    return qkv[:, :, 0], qkv[:, :, 1], qkv[:, :, 2]


def stick_breaking(q, k, v, bias, q_pos, k_pos):
    z = jnp.einsum("nqhd,nkhd->nhqk", q, k).astype(jnp.float32) * (1.0 / math.sqrt(SB_HEAD_DIM))
    z = z + bias.astype(jnp.float32)[None, :, None, None]
    mask = k_pos[None, :] < q_pos[:, None]
    log_1m = jnp.where(mask, jax.nn.log_sigmoid(-z), 0.0)
    tail = lax.cumsum(log_1m, axis=3, reverse=True) - log_1m
    a = jnp.where(mask, jnp.exp(jax.nn.log_sigmoid(z) + tail), 0.0)
    return jnp.einsum("nhqk,nkhd->nqhd", a, v.astype(jnp.float32))


def sb_prompt(x, w_qkv, bias, w_o):
    n, t, _ = x.shape
    q, k, v = qkv_heads(x, w_qkv)
    pos = jnp.arange(t)
    blocks = []
    for lo in range(0, t, Q_BLOCK):
        hi = min(lo + Q_BLOCK, t)
        blocks.append(stick_breaking(q[:, lo:hi], k[:, :hi], v[:, :hi], bias, pos[lo:hi], pos[:hi]))
    o = jnp.concatenate(blocks, axis=1).reshape(n, t, D_MODEL).astype(x.dtype)
    return o @ w_o, k, v


def sb_sample(x, cache_k, cache_v, page_table, w_qkv, bias, w_o):
    n, t, _ = x.shape
    q, k_new, v_new = qkv_heads(x, w_qkv)
    k_past = cache_k[page_table].reshape(n, -1, SB_HEADS, SB_HEAD_DIM)
    v_past = cache_v[page_table].reshape(n, -1, SB_HEADS, SB_HEAD_DIM)
    past = k_past.shape[1]
    k_all = jnp.concatenate([k_past, k_new.astype(k_past.dtype)], axis=1)
    v_all = jnp.concatenate([v_past, v_new.astype(v_past.dtype)], axis=1)
    k_pos = jnp.arange(past + t)
    q_pos = past + jnp.arange(t)
    o = stick_breaking(q, k_all, v_all, bias, q_pos, k_pos).reshape(n, t, D_MODEL).astype(x.dtype)
    return o @ w_o, k_new, v_new


def swiglu(x, w_gu, w_down):
    g, u = jnp.split(x @ w_gu, 2, axis=-1)
    return (jax.nn.silu(g) * u) @ w_down


def moe_ffn(x, w_router, w_gu, w_down):
    logits = (x @ w_router).astype(jnp.float32)
    top_val, top_idx = lax.top_k(logits, TOP_K)
    gates = jax.nn.softmax(top_val, axis=-1)
    combine = jnp.sum(jax.nn.one_hot(top_idx, N_EXPERTS, dtype=jnp.float32) * gates[..., None], axis=-2)
    y = jnp.zeros_like(x)
    for e in range(N_EXPERTS):
        y = y + combine[..., e:e + 1].astype(x.dtype) * swiglu(x, w_gu[e], w_down[e])
    return y


def setup_inputs(seed: int = 0) -> dict:
    key = jax.random.key(seed)
    ks = jax.random.split(key, 24)
    n_pages = PAST_LEN // PAGE_SIZE
    n_used = DEC_BATCH * n_pages
    n_pool = (n_used * 5 + 3) // 4

    def nrm(k, shape, scale):
        return scale * jax.random.normal(k, shape, jnp.float32)

    x_prompt = nrm(ks[0], (BATCH, SEQ, D_MODEL), 1.0)
    x_sample = nrm(ks[1], (DEC_BATCH, DEC_SEQ, D_MODEL), 1.0)
    cache_shape = (N_SB_LAYERS, n_pool, PAGE_SIZE, SB_HEADS, SB_HEAD_DIM)
    cache_k = nrm(ks[2], cache_shape, 1.0)
    cache_v = nrm(ks[3], cache_shape, 1.0)
    page_table = jax.random.permutation(ks[4], n_pool)[:n_used].reshape(DEC_BATCH, n_pages).astype(jnp.int32)

    ln_g = 1.0 + nrm(ks[5], (DEPTH, 2, D_MODEL), 0.05)
    ln_b = nrm(ks[6], (DEPTH, 2, D_MODEL), 0.02)

    gm_w_uv = nrm(ks[7], (N_GM_LAYERS, D_MODEL, 2 * GM_WIDTH), D_MODEL ** -0.5)
    gm_vn_g = 1.0 + nrm(ks[8], (N_GM_LAYERS, GM_WIDTH), 0.05)
    gm_vn_b = nrm(ks[9], (N_GM_LAYERS, GM_WIDTH), 0.02)
    gm_w_s = nrm(ks[10], (N_GM_LAYERS, GM_GROUPS, CHUNK, CHUNK), 0.5 * CHUNK ** -0.5)
    gm_b_s = 1.0 + nrm(ks[11], (N_GM_LAYERS, GM_GROUPS, CHUNK), 0.1)
    gm_w_o = nrm(ks[12], (N_GM_LAYERS, GM_WIDTH, D_MODEL), BETA_INIT * GM_WIDTH ** -0.5)

    v_scale = jnp.concatenate([jnp.ones((2 * D_MODEL,), jnp.float32), jnp.full((D_MODEL,), BETA_INIT, jnp.float32)])
    sb_w_qkv = nrm(ks[13], (N_SB_LAYERS, D_MODEL, 3 * D_MODEL), D_MODEL ** -0.5) * v_scale
    head_span = jnp.linspace(SB_BIAS_HI, SB_BIAS_LO, SB_HEADS, dtype=jnp.float32)
    sb_bias = head_span[None, :] + nrm(ks[20], (N_SB_LAYERS, SB_HEADS), 0.1)
    sb_w_o = nrm(ks[14], (N_SB_LAYERS, D_MODEL, D_MODEL), BETA_INIT * D_MODEL ** -0.5)

    ffn_w_gu = nrm(ks[15], (N_DENSE_LAYERS, D_MODEL, 2 * D_FF), D_MODEL ** -0.5)
    ffn_w_down = nrm(ks[16], (N_DENSE_LAYERS, D_FF, D_MODEL), BETA_INIT * D_FF ** -0.5)

    moe_w_router = nrm(ks[17], (N_MOE_LAYERS, D_MODEL, N_EXPERTS), D_MODEL ** -0.5)
    moe_w_gu = nrm(ks[18], (N_MOE_LAYERS, N_EXPERTS, D_MODEL, 2 * D_FF_EXPERT), D_MODEL ** -0.5)
    moe_w_down = nrm(ks[19], (N_MOE_LAYERS, N_EXPERTS, D_FF_EXPERT, D_MODEL), BETA_INIT * D_FF_EXPERT ** -0.5)

    return {"x_prompt": x_prompt, "x_sample": x_sample, "cache_k": cache_k, "cache_v": cache_v,
            "page_table": page_table, "ln_g": ln_g, "ln_b": ln_b,
            "gm_w_uv": gm_w_uv, "gm_vn_g": gm_vn_g, "gm_vn_b": gm_vn_b, "gm_w_s": gm_w_s,
            "gm_b_s": gm_b_s, "gm_w_o": gm_w_o, "sb_w_qkv": sb_w_qkv, "sb_bias": sb_bias, "sb_w_o": sb_w_o,
            "ffn_w_gu": ffn_w_gu, "ffn_w_down": ffn_w_down, "moe_w_router": moe_w_router,
            "moe_w_gu": moe_w_gu, "moe_w_down": moe_w_down}


def reference(x_prompt, x_sample, cache_k, cache_v, page_table, ln_g, ln_b,
              gm_w_uv, gm_vn_g, gm_vn_b, gm_w_s, gm_b_s, gm_w_o, sb_w_qkv, sb_bias, sb_w_o,
              ffn_w_gu, ffn_w_down, moe_w_router, moe_w_gu, moe_w_down):
    xp, xs = x_prompt, x_sample
    k_prompt, v_prompt, k_sample, v_sample, gm_v_sample = [], [], [], [], []
    for i in range(DEPTH):
        j = i // N_MIXERS
        if i % N_MIXERS == 0:
            mp, _ = gmlp_mixer(xp, gm_w_uv[j], gm_vn_g[j], gm_vn_b[j], gm_w_s[j], gm_b_s[j], gm_w_o[j], CHUNK)
            ms, vs_rows = gmlp_mixer(xs, gm_w_uv[j], gm_vn_g[j], gm_vn_b[j], gm_w_s[j], gm_b_s[j], gm_w_o[j], xs.shape[1])
            gm_v_sample.append(vs_rows)
        else:
            mp, kp, vp = sb_prompt(xp, sb_w_qkv[j], sb_bias[j], sb_w_o[j])
            ms, kn, vn = sb_sample(xs, cache_k[j], cache_v[j], page_table, sb_w_qkv[j], sb_bias[j], sb_w_o[j])
            k_prompt.append(kp)
            v_prompt.append(vp)
            k_sample.append(kn)
            v_sample.append(vn)
        xp = layer_norm(ALPHA * xp + mp, ln_g[i, 0], ln_b[i, 0])
        xs = layer_norm(ALPHA * xs + ms, ln_g[i, 0], ln_b[i, 0])
        f = i // 2
        if i % 2 == 0:
            fp = swiglu(xp, ffn_w_gu[f], ffn_w_down[f])
            fs = swiglu(xs, ffn_w_gu[f], ffn_w_down[f])
        else:
            fp = moe_ffn(xp, moe_w_router[f], moe_w_gu[f], moe_w_down[f])
            fs = moe_ffn(xs, moe_w_router[f], moe_w_gu[f], moe_w_down[f])
        xp = layer_norm(ALPHA * xp + fp, ln_g[i, 1], ln_b[i, 1])
        xs = layer_norm(ALPHA * xs + fs, ln_g[i, 1], ln_b[i, 1])
    return (xp, xs, jnp.stack(k_prompt), jnp.stack(v_prompt), jnp.stack(k_sample), jnp.stack(v_sample), jnp.stack(gm_v_sample))
```

```python
import functools
import math

import jax
import jax.numpy as jnp
from jax import lax
from jax.experimental import pallas as pl
from jax.experimental.pallas import tpu as pltpu

F32 = jnp.float32
BF16 = jnp.bfloat16
LN_EPS = 1e-5
CHUNK = 128
PAGE = 128
LANES = 128
V7X_VMEM_LIMIT_BYTES = 56 << 20
PAGES_PER_STEP = 4


def _params(*sem):
    return pltpu.CompilerParams(dimension_semantics=sem, vmem_limit_bytes=V7X_VMEM_LIMIT_BYTES)


def _ln_rows(x, g, b):
    mu = jnp.mean(x, axis=-1, keepdims=True)
    xc = x - mu
    var = jnp.mean(xc * xc, axis=-1, keepdims=True)
    return xc * lax.rsqrt(var + LN_EPS) * g + b


def _pick_tile(n, pref):
    t = pref
    while n % t:
        t //= 2
    return t


def _gmlp_kernel(x_ref, wuv_ref, vng_ref, vnb_ref, ws_ref, bs_ref, wo_ref, lng_ref, lnb_ref,
                 x1_ref, x1b_ref, vs_ref, y_ref, *, alpha, width, groups, n_prompt_tiles):
    tm = x_ref.shape[0]
    x = x_ref[...]
    h = jnp.dot(x.astype(BF16), wuv_ref[...], preferred_element_type=F32)
    h = 0.5 * h * (1.0 + lax.erf(h * (1.0 / math.sqrt(2.0))))
    u = h[:, :width]
    v = _ln_rows(h[:, width:], vng_ref[...], vnb_ref[...])

    @pl.when(pl.program_id(0) >= n_prompt_tiles)
    def _():
        vs_ref[...] = v

    vb = v.astype(BF16)
    gd = width // groups
    for c in range(tm // CHUNK):
        r0 = c * CHUNK
        for g in range(groups):
            c0 = g * gd
            s = jnp.dot(ws_ref[g], vb[r0:r0 + CHUNK, c0:c0 + gd], preferred_element_type=F32)
            s = s + bs_ref[:, c0:c0 + gd]
            y_ref[r0:r0 + CHUNK, c0:c0 + gd] = (u[r0:r0 + CHUNK, c0:c0 + gd] * s).astype(BF16)
    m = jnp.dot(y_ref[...], wo_ref[...], preferred_element_type=F32)
    x1 = _ln_rows(alpha * x + m, lng_ref[...], lnb_ref[...])
    x1_ref[...] = x1
    x1b_ref[...] = x1.astype(BF16)


def _gmlp_layer(x, n_prompt, w_uv, vn_g, vn_b, ws2, bs2, w_o, ln_g, ln_b, alpha, groups):
    t, d = x.shape
    width = w_o.shape[0]
    tm = 2 * CHUNK
    npt = n_prompt // tm
    nst = (t - n_prompt) // tm
    row = lambda i: (i, 0)
    const2 = lambda i: (0, 0)
    variant = lambda i: (jnp.where(i >= npt, 1, 0), 0, 0, 0)
    variant3 = lambda i: (jnp.where(i >= npt, 1, 0), 0, 0)
    kern = functools.partial(_gmlp_kernel, alpha=alpha, width=width, groups=groups, n_prompt_tiles=npt)
    return pl.pallas_call(
        kern,
        grid=(npt + nst,),
        in_specs=[
            pl.BlockSpec((tm, d), row),
            pl.BlockSpec((d, 2 * width), const2),
            pl.BlockSpec((1, width), const2),
            pl.BlockSpec((1, width), const2),
            pl.BlockSpec((None, groups, CHUNK, CHUNK), variant),
            pl.BlockSpec((None, CHUNK, width), variant3),
            pl.BlockSpec((width, d), const2),
            pl.BlockSpec((1, d), const2),
            pl.BlockSpec((1, d), const2),
        ],
        out_specs=[
            pl.BlockSpec((tm, d), row),
            pl.BlockSpec((tm, d), row),
            pl.BlockSpec((tm, width), lambda i: (jnp.maximum(i - npt, 0), 0)),
        ],
        out_shape=[
            jax.ShapeDtypeStruct((t, d), F32),
            jax.ShapeDtypeStruct((t, d), BF16),
            jax.ShapeDtypeStruct((t - n_prompt, width), F32),
        ],
        scratch_shapes=[pltpu.VMEM((tm, width), BF16)],
        compiler_params=_params("arbitrary"),
        name="gmlp_layer",
    )(x, w_uv, vn_g, vn_b, ws2, bs2, w_o, ln_g, ln_b)


def _swiglu_up_kernel(te_ref, ts_ref, nu_ref, x_ref, wg_ref, wu_ref, h_ref):
    @pl.when(pl.program_id(1) < nu_ref[0])
    def _():
        x = x_ref[...]
        g = jnp.dot(x, wg_ref[...], preferred_element_type=F32)
        u = jnp.dot(x, wu_ref[...], preferred_element_type=F32)
        h_ref[...] = (g * (1.0 / (1.0 + jnp.exp(-g))) * u).astype(BF16)


def _swiglu_up(xb, w_gu, tile_expert, tile_src, n_used, tm):
    d = xb.shape[1]
    f = w_gu.shape[2] // 2
    tf = f // 2
    nf = f // tf
    n_tiles = tile_expert.shape[0]
    return pl.pallas_call(
        _swiglu_up_kernel,
        grid_spec=pltpu.PrefetchScalarGridSpec(
            num_scalar_prefetch=3,
            grid=(nf, n_tiles),
            in_specs=[
                pl.BlockSpec((tm, d), lambda j, t, te, ts, nu: (ts[t], 0)),
                pl.BlockSpec((None, d, tf), lambda j, t, te, ts, nu: (te[t], 0, j)),
                pl.BlockSpec((None, d, tf), lambda j, t, te, ts, nu: (te[t], 0, nf + j)),
            ],
            out_specs=pl.BlockSpec((tm, tf), lambda j, t, te, ts, nu: (jnp.minimum(t, nu[0] - 1), j)),
        ),
        out_shape=jax.ShapeDtypeStruct((n_tiles * tm, f), BF16),
        compiler_params=_params("arbitrary", "arbitrary"),
        name="swiglu_up",
    )(tile_expert, tile_src, n_used, xb, w_gu, w_gu)


def _down_ln_kernel(h_ref, wd_ref, xres_ref, lng_ref, lnb_ref, x2_ref, x2b_ref, *, alpha):
    m = jnp.dot(h_ref[...], wd_ref[...], preferred_element_type=F32)
    x2 = _ln_rows(alpha * xres_ref[...] + m, lng_ref[...], lnb_ref[...])
    x2_ref[...] = x2
    x2b_ref[...] = x2.astype(BF16)


def _down_ln(hb, w_down, xres, ln_g, ln_b, alpha, tm):
    t, f = hb.shape
    d = w_down.shape[1]
    row = lambda i: (i, 0)
    const2 = lambda i: (0, 0)
    return pl.pallas_call(
        functools.partial(_down_ln_kernel, alpha=alpha),
        grid=(t // tm,),
        in_specs=[
            pl.BlockSpec((tm, f), row),
            pl.BlockSpec((f, d), const2),
            pl.BlockSpec((tm, d), row),
            pl.BlockSpec((1, d), const2),
            pl.BlockSpec((1, d), const2),
        ],
        out_specs=[pl.BlockSpec((tm, d), row), pl.BlockSpec((tm, d), row)],
        out_shape=[jax.ShapeDtypeStruct((t, d), F32), jax.ShapeDtypeStruct((t, d), BF16)],
        compiler_params=_params("arbitrary"),
        name="down_ln",
    )(hb, w_down, xres, ln_g, ln_b)


def _down_gate_kernel(te_ref, nu_ref, h_ref, wd_ref, gate_ref, y_ref):
    @pl.when(pl.program_id(0) < nu_ref[0])
    def _():
        m = jnp.dot(h_ref[...], wd_ref[...], preferred_element_type=F32)
        y_ref[...] = (m * gate_ref[:, 0:1]).astype(BF16)


def _down_gate(hg, w_down, gate_rows, tile_expert, n_used, tm):
    f, d = w_down.shape[1], w_down.shape[2]
    n_tiles = tile_expert.shape[0]
    clamp = lambda t, te, nu: (jnp.minimum(t, nu[0] - 1), 0)
    return pl.pallas_call(
        _down_gate_kernel,
        grid_spec=pltpu.PrefetchScalarGridSpec(
            num_scalar_prefetch=2,
            grid=(n_tiles,),
            in_specs=[
                pl.BlockSpec((tm, f), clamp),
                pl.BlockSpec((None, f, d), lambda t, te, nu: (te[t], 0, 0)),
                pl.BlockSpec((tm, LANES), clamp),
            ],
            out_specs=pl.BlockSpec((tm, d), clamp),
        ),
        out_shape=jax.ShapeDtypeStruct((n_tiles * tm, d), BF16),
        compiler_params=_params("arbitrary"),
        name="down_gate",
    )(tile_expert, n_used, hg, w_down, gate_rows)


def _split_bf16(a):
    hi = a.astype(BF16)
    lo = (a - hi.astype(F32)).astype(BF16)
    return hi, lo


def _router_kernel(x_ref, wr_ref, tri_ref, info_ref, base_ref, cnt_ref, *, n_experts):
    i = pl.program_id(0)

    @pl.when(i == 0)
    def _():
        cnt_ref[...] = jnp.zeros_like(cnt_ref)

    xh, xl = _split_bf16(x_ref[...])
    wh, wl = _split_bf16(wr_ref[...])
    logits = (jnp.dot(xh, wh, preferred_element_type=F32) + jnp.dot(xl, wh, preferred_element_type=F32)
              + jnp.dot(xh, wl, preferred_element_type=F32))
    tm = logits.shape[0]
    lane = lax.broadcasted_iota(jnp.int32, (tm, LANES), 1).astype(F32)
    neg = jnp.float32(-jnp.inf)
    lg = jnp.where(lane < n_experts, logits, neg)
    m1 = jnp.max(lg, axis=1, keepdims=True)
    i1 = jnp.min(jnp.where(lg == m1, lane, float(LANES)), axis=1, keepdims=True)
    lg2 = jnp.where(lane == i1, neg, lg)
    m2 = jnp.max(lg2, axis=1, keepdims=True)
    i2 = jnp.min(jnp.where(lg2 == m2, lane, float(LANES)), axis=1, keepdims=True)
    dd = jnp.exp(m2 - m1)
    g1 = 1.0 / (1.0 + dd)
    g2 = dd / (1.0 + dd)
    oh1 = lane == i1
    oh2 = lane == i2
    both = jnp.where(oh1 | oh2, 1.0, 0.0)
    prefix = jnp.dot(tri_ref[...], both.astype(BF16), preferred_element_type=F32)
    base = cnt_ref[0:1, :]
    rank_all = prefix + base
    r1 = jnp.sum(jnp.where(oh1, rank_all, 0.0), axis=1, keepdims=True)
    r2 = jnp.sum(jnp.where(oh2, rank_all, 0.0), axis=1, keepdims=True)
    info = jnp.where(lane == 0, i1, 0.0)
    info = jnp.where(lane == 1, i2, info)
    info = jnp.where(lane == 2, g1, info)
    info = jnp.where(lane == 3, g2, info)
    info = jnp.where(lane == 4, r1, info)
    info = jnp.where(lane == 5, r2, info)
    info_ref[...] = info
    base_ref[...] = base
    cnt_ref[0:1, :] = base + jnp.sum(both, axis=0, keepdims=True)


def _router(x1, w_router, tc):
    t, d = x1.shape
    n_experts = w_router.shape[1]
    wr = jnp.pad(w_router, ((0, 0), (0, LANES - n_experts)))
    tri = (jnp.arange(tc)[:, None] > jnp.arange(tc)[None, :]).astype(BF16)
    n_chunks = t // tc
    info, base, cnt = pl.pallas_call(
        functools.partial(_router_kernel, n_experts=n_experts),
        grid=(n_chunks,),
        in_specs=[
            pl.BlockSpec((tc, d), lambda i: (i, 0)),
            pl.BlockSpec((d, LANES), lambda i: (0, 0)),
            pl.BlockSpec((tc, tc), lambda i: (0, 0)),
        ],
        out_specs=[
            pl.BlockSpec((tc, LANES), lambda i: (i, 0)),
            pl.BlockSpec((None, 1, LANES), lambda i: (i, 0, 0)),
            pl.BlockSpec((8, LANES), lambda i: (0, 0)),
        ],
        out_shape=[
            jax.ShapeDtypeStruct((t, LANES), F32),
            jax.ShapeDtypeStruct((n_chunks, 1, LANES), F32),
            jax.ShapeDtypeStruct((8, LANES), F32),
        ],
        compiler_params=_params("arbitrary"),
        name="router",
    )(x1, wr, tri)
    return info, base[:, 0, :n_experts].astype(jnp.int32), cnt[0, :n_experts].astype(jnp.int32)


def _dispatch_kernel(vt_ref, vc_ref, vf_ref, vok_ref, x_ref, p1_ref, p2_ref, g1_ref, g2_ref,
                     xg_ref, gg_ref):
    v = pl.program_id(0)
    tmg = xg_ref.shape[0]

    @pl.when(vf_ref[v] == 1)
    def _():
        xg_ref[...] = jnp.zeros_like(xg_ref)
        gg_ref[...] = jnp.zeros_like(gg_ref)

    @pl.when(vok_ref[v] == 1)
    def _():
        tc = x_ref.shape[0]
        rows = vt_ref[v] * tmg + lax.broadcasted_iota(jnp.int32, (tmg, tc), 0)
        sel1 = p1_ref[...] == rows
        sel2 = p2_ref[...] == rows
        onehot = jnp.where(sel1 | sel2, 1.0, 0.0).astype(BF16)
        xg_ref[...] += jnp.dot(onehot, x_ref[...], preferred_element_type=F32).astype(BF16)
        gate = jnp.where(sel1, g1_ref[...], 0.0) + jnp.where(sel2, g2_ref[...], 0.0)
        gg_ref[...] += jnp.broadcast_to(jnp.sum(gate, axis=1, keepdims=True), gg_ref.shape)


def _dispatch(xb, pos1, pos2, g1, g2, visits, n_tiles, tmg, tc):
    t, d = xb.shape
    n_chunks = t // tc
    vt, vc, vf, vok = visits
    as_rows = lambda a: a.reshape(n_chunks, 1, tc)
    chunk3 = pl.BlockSpec((None, 1, tc), lambda v, vt, vc, vf, vok: (vc[v], 0, 0))
    tile = lambda v, vt, vc, vf, vok: (vt[v], 0)
    return pl.pallas_call(
        _dispatch_kernel,
        grid_spec=pltpu.PrefetchScalarGridSpec(
            num_scalar_prefetch=4,
            grid=(vt.shape[0],),
            in_specs=[pl.BlockSpec((tc, d), lambda v, vt, vc, vf, vok: (vc[v], 0)),
                      chunk3, chunk3, chunk3, chunk3],
            out_specs=[pl.BlockSpec((tmg, d), tile), pl.BlockSpec((tmg, LANES), tile)],
        ),
        out_shape=[jax.ShapeDtypeStruct((n_tiles * tmg, d), BF16),
                   jax.ShapeDtypeStruct((n_tiles * tmg, LANES), F32)],
        compiler_params=_params("arbitrary"),
        name="moe_dispatch",
    )(vt, vc, vf, vok, xb, as_rows(pos1), as_rows(pos2), as_rows(g1), as_rows(g2))


def _combine_kernel(vt_ref, vc_ref, vf_ref, vl_ref, vok_ref, yg_ref, p1_ref, p2_ref, xres_ref,
                    lng_ref, lnb_ref, x2_ref, x2b_ref, acc_ref, *, alpha):
    v = pl.program_id(0)
    tmg = yg_ref.shape[0]

    @pl.when(vf_ref[v] == 1)
    def _():
        acc_ref[...] = jnp.zeros_like(acc_ref)

    @pl.when(vok_ref[v] == 1)
    def _():
        tc = acc_ref.shape[0]
        rows = vt_ref[v] * tmg + lax.broadcasted_iota(jnp.int32, (tc, tmg), 1)
        onehot = jnp.where((p1_ref[...] == rows) | (p2_ref[...] == rows), 1.0, 0.0).astype(BF16)
        acc_ref[...] += jnp.dot(onehot, yg_ref[...], preferred_element_type=F32)

    @pl.when(vl_ref[v] == 1)
    def _():
        x2 = _ln_rows(alpha * xres_ref[...] + acc_ref[...], lng_ref[...], lnb_ref[...])
        x2_ref[...] = x2
        x2b_ref[...] = x2.astype(BF16)


def _combine(yg, pos1, pos2, xres, ln_g, ln_b, visits, alpha, tmg, tc):
    t, d = xres.shape
    vt, vc, vf, vl, vok = visits
    chunk = lambda v, vt, vc, vf, vl, vok: (vc[v], 0)
    const2 = lambda v, vt, vc, vf, vl, vok: (0, 0)
    return pl.pallas_call(
        functools.partial(_combine_kernel, alpha=alpha),
        grid_spec=pltpu.PrefetchScalarGridSpec(
            num_scalar_prefetch=5,
            grid=(vt.shape[0],),
            in_specs=[
                pl.BlockSpec((tmg, d), lambda v, vt, vc, vf, vl, vok: (vt[v], 0)),
                pl.BlockSpec((tc, 1), chunk),
                pl.BlockSpec((tc, 1), chunk),
                pl.BlockSpec((tc, d), chunk),
                pl.BlockSpec((1, d), const2),
                pl.BlockSpec((1, d), const2),
            ],
            out_specs=[pl.BlockSpec((tc, d), chunk), pl.BlockSpec((tc, d), chunk)],
            scratch_shapes=[pltpu.VMEM((tc, d), F32)],
        ),
        out_shape=[jax.ShapeDtypeStruct((t, d), F32), jax.ShapeDtypeStruct((t, d), BF16)],
        compiler_params=_params("arbitrary"),
        name="moe_combine",
    )(vt, vc, vf, vl, vok, yg, pos1.reshape(t, 1), pos2.reshape(t, 1), xres, ln_g, ln_b)


def _expand_visits(lo, hi, active, n_max):
    n = jnp.where(active, hi - lo + 1, 0)
    end = jnp.cumsum(n)
    start = end - n
    total = end[-1]
    v = jnp.arange(n_max, dtype=jnp.int32)
    ok = v < total
    vv = jnp.minimum(v, total - 1)
    owner = jnp.searchsorted(end, vv, side="right").astype(jnp.int32)
    idx = lo[owner] + (vv - start[owner])
    first = ok & (vv == start[owner])
    last = ok & (vv == end[owner] - 1)
    i32 = lambda a: a.astype(jnp.int32)
    return owner, i32(idx), i32(first), i32(last), i32(ok)


def _moe_layer(x1, x1b, w_router, w_gu, w_down, ln_g, ln_b, alpha, tmg, tc):
    t, d = x1.shape
    n_experts = w_router.shape[1]
    n_chunks = t // tc
    info, base, counts = _router(x1, w_router, tc)
    e1 = info[:, 0].astype(jnp.int32)
    e2 = info[:, 1].astype(jnp.int32)
    g1, g2 = info[:, 2], info[:, 3]
    r1 = info[:, 4].astype(jnp.int32)
    r2 = info[:, 5].astype(jnp.int32)

    n_tiles = (2 * t) // tmg + n_experts
    tiles_e = (counts + tmg - 1) // tmg
    tile_end = jnp.cumsum(tiles_e)
    tile_start = tile_end - tiles_e
    n_used = tile_end[-1]
    off = tile_start * tmg
    pos1 = (off[e1] + r1).astype(jnp.int32)
    pos2 = (off[e2] + r2).astype(jnp.int32)
    tid = jnp.arange(n_tiles, dtype=jnp.int32)
    tile_expert = jnp.minimum(jnp.searchsorted(tile_end, jnp.minimum(tid, n_used - 1), side="right"),
                              n_experts - 1).astype(jnp.int32)

    cum_incl = jnp.concatenate([base[1:], counts[None, :]], axis=0)
    k = tid - tile_start[tile_expert]
    lo_rank = k * tmg
    hi_rank = jnp.minimum(lo_rank + tmg, counts[tile_expert]) - 1
    cum_t = cum_incl[:, tile_expert]
    c_lo = jnp.sum(cum_t <= lo_rank[None, :], axis=0).astype(jnp.int32)
    c_hi = jnp.sum(cum_t <= hi_rank[None, :], axis=0).astype(jnp.int32)
    n_visits = n_experts * n_chunks + n_tiles
    d_owner, d_idx, d_first, _, d_ok = _expand_visits(c_lo, c_hi, tid < n_used, n_visits)
    xg, gate_rows = _dispatch(x1b, pos1, pos2, g1, g2, (d_owner, d_idx, d_first, d_ok), n_tiles, tmg, tc)

    n_used_arr = n_used.reshape(1).astype(jnp.int32)
    hg = _swiglu_up(xg, w_gu, tile_expert, jnp.minimum(tid, n_used - 1), n_used_arr, tmg)
    yg = _down_gate(hg, w_down, gate_rows, tile_expert, n_used_arr, tmg)

    cnt_ce = cum_incl - base
    t_lo = (off[None, :] + base) // tmg
    t_hi = (off[None, :] + cum_incl - 1) // tmg
    c_owner, c_idx, c_first, c_last, c_ok = _expand_visits(
        t_lo.reshape(-1).astype(jnp.int32), t_hi.reshape(-1).astype(jnp.int32),
        (cnt_ce > 0).reshape(-1), n_visits)
    chunk_of = c_owner // n_experts
    prev_chunk = jnp.concatenate([jnp.full((1,), -1, jnp.int32), chunk_of[:-1]])
    next_chunk = jnp.concatenate([chunk_of[1:], jnp.full((1,), -1, jnp.int32)])
    next_ok = jnp.concatenate([c_ok[1:], jnp.zeros((1,), jnp.int32)])
    first = (c_ok == 1) & (chunk_of != prev_chunk)
    last = (c_ok == 1) & ((chunk_of != next_chunk) | (next_ok == 0))
    visits = (c_idx, chunk_of.astype(jnp.int32), first.astype(jnp.int32), last.astype(jnp.int32), c_ok)
    return _combine(yg, pos1, pos2, x1, ln_g, ln_b, visits, alpha, tmg, tc)


def _qkv_kernel(x_ref, w_ref, q_ref, k_ref, v_ref, kb_ref, vb_ref, *, d, scale):
    acc = jnp.dot(x_ref[...], w_ref[...], preferred_element_type=F32)
    k = acc[:, d:2 * d]
    v = acc[:, 2 * d:]
    q_ref[...] = (acc[:, :d] * scale).astype(BF16)
    k_ref[...] = k
    v_ref[...] = v
    kb_ref[...] = k.astype(BF16)
    vb_ref[...] = v.astype(BF16)


def _qkv(xb, w_qkv, scale, tm):
    t, d = xb.shape
    row = lambda i: (i, 0)
    spec = pl.BlockSpec((tm, d), row)
    return pl.pallas_call(
        functools.partial(_qkv_kernel, d=d, scale=scale),
        grid=(t // tm,),
        in_specs=[spec, pl.BlockSpec((d, 3 * d), lambda i: (0, 0))],
        out_specs=[spec] * 5,
        out_shape=[jax.ShapeDtypeStruct((t, d), BF16), jax.ShapeDtypeStruct((t, d), F32),
                   jax.ShapeDtypeStruct((t, d), F32), jax.ShapeDtypeStruct((t, d), BF16),
                   jax.ShapeDtypeStruct((t, d), BF16)],
        compiler_params=_params("arbitrary"),
        name="qkv",
    )(xb, w_qkv)


def _sb_block(z, mask, carry, acc, v_blk, u_tri):
    lg = jnp.log(1.0 + jnp.exp(-jnp.abs(z)))
    sp = jnp.maximum(z, 0.0) + lg
    arg = jnp.minimum(z, 0.0) - lg
    if mask is not None:
        sp = jnp.where(mask, sp, 0.0)
    tail = jnp.dot(sp.astype(BF16), u_tri, preferred_element_type=F32)
    a = jnp.exp(arg - tail - carry)
    if mask is not None:
        a = jnp.where(mask, a, 0.0)
    acc = acc + jnp.dot(a.astype(BF16), v_blk, preferred_element_type=F32)
    carry = carry + jnp.sum(sp, axis=1, keepdims=True)
    return carry, acc


def _sb_prompt_kernel(q_ref, k_ref, v_ref, b_ref, u_ref, o_ref, *, tq, dh):
    s = q_ref.shape[0]
    lane = lax.broadcasted_iota(jnp.int32, (1, LANES), 1)
    diag = lax.broadcasted_iota(jnp.int32, (tq, tq), 1) < lax.broadcasted_iota(jnp.int32, (tq, tq), 0)
    u_tri = u_ref[...]
    nt = (((1,), (1,)), ((), ()))

    def q_body(qi, _):
        qoff = pl.multiple_of(qi * tq, tq)
        q2 = q_ref[pl.ds(qoff, tq), :]
        outs = []
        for hh in range(LANES // dh):
            in_head = (lane >= hh * dh) & (lane < (hh + 1) * dh)
            qh = jnp.where(in_head, q2, jnp.zeros_like(q2))
            bh = b_ref[:, hh * dh:hh * dh + 1]

            def blk(kb, mask, carry, acc):
                koff = pl.multiple_of(kb * tq, tq)
                z = lax.dot_general(qh, k_ref[pl.ds(koff, tq), :], nt, preferred_element_type=F32) + bh
                return _sb_block(z, mask, carry, acc, v_ref[pl.ds(koff, tq), :], u_tri)

            carry, acc = blk(qi, diag, jnp.zeros((tq, 1), F32), jnp.zeros((tq, LANES), F32))
            carry, acc = lax.fori_loop(0, qi, lambda i, ca: blk(qi - 1 - i, None, *ca), (carry, acc))
            outs.append(acc)
        o = outs[0]
        for hh in range(1, len(outs)):
            o = jnp.where(lane >= hh * dh, outs[hh], o)
        o_ref[pl.ds(qoff, tq), :] = o.astype(BF16)
        return 0

    lax.fori_loop(0, s // tq, q_body, 0)


def _sb_prompt(q, kb, vb, bias_lanes, n_batch, seq, dh, tq):
    d = q.shape[1]
    u_tri = (jnp.arange(tq)[:, None] > jnp.arange(tq)[None, :]).astype(BF16)
    blk = pl.BlockSpec((seq, LANES), lambda n, hp: (n, hp))
    return pl.pallas_call(
        functools.partial(_sb_prompt_kernel, tq=tq, dh=dh),
        grid=(n_batch, d // LANES),
        in_specs=[blk, blk, blk,
                  pl.BlockSpec((1, LANES), lambda n, hp: (0, hp)),
                  pl.BlockSpec((tq, tq), lambda n, hp: (0, 0))],
        out_specs=blk,
        out_shape=jax.ShapeDtypeStruct((n_batch * seq, d), BF16),
        compiler_params=_params("arbitrary", "arbitrary"),
        name="sb_prompt",
    )(q, kb, vb, bias_lanes, u_tri)


def _sb_sample_kernel(pt_ref, q_ref, kn_ref, vn_ref, b_ref, u_ref, *refs, n_heads, dh, n_groups):
    kp_refs = refs[:PAGES_PER_STEP]
    vp_refs = refs[PAGES_PER_STEP:2 * PAGES_PER_STEP]
    o_ref, carry_ref, acc_ref = refs[2 * PAGES_PER_STEP:]
    pg = pl.program_id(1)
    u_tri = u_ref[...]
    nt = (((1,), (1,)), ((), ()))
    rows = q_ref.shape[1]

    @pl.when(pg == 0)
    def _():
        mask = (lax.broadcasted_iota(jnp.int32, (rows, PAGE), 1)
                < lax.broadcasted_iota(jnp.int32, (rows, PAGE), 0))
        for h in range(n_heads):
            z = lax.dot_general(q_ref[h], kn_ref[h], nt, preferred_element_type=F32) + b_ref[:, h * dh:h * dh + 1]
            carry, acc = _sb_block(z, mask, jnp.zeros((rows, 1), F32), jnp.zeros((rows, dh), F32),
                                   vn_ref[h], u_tri)
            carry_ref[h] = jnp.broadcast_to(carry, (rows, LANES))
            acc_ref[h] = acc

    for i in range(PAGES_PER_STEP):
        for h in range(n_heads):
            k_h = kp_refs[i][pl.ds(h, PAGE, stride=n_heads), :].astype(BF16)
            v_h = vp_refs[i][pl.ds(h, PAGE, stride=n_heads), :].astype(BF16)
            z = lax.dot_general(q_ref[h], k_h, nt, preferred_element_type=F32) + b_ref[:, h * dh:h * dh + 1]
            carry, acc = _sb_block(z, None, carry_ref[h][:, 0:1], acc_ref[h], v_h, u_tri)
            carry_ref[h] = jnp.broadcast_to(carry, (rows, LANES))
            acc_ref[h] = acc

    @pl.when(pg == n_groups - 1)
    def _():
        o_ref[...] = acc_ref[...]


def _sb_sample(qh, kn, vn, bias_lanes, cache_k, cache_v, page_table, layer):
    n_seq, n_heads, rows, dh = qh.shape
    n_pages = page_table.shape[1]
    n_groups = n_pages // PAGES_PER_STEP
    u_tri = (jnp.arange(PAGE)[:, None] > jnp.arange(PAGE)[None, :]).astype(BF16)
    seq4 = lambda n, pg, pt: (n, 0, 0, 0)

    def page_spec(i):
        def imap(n, pg, pt):
            return (layer, pt[n * n_pages + (n_pages - 1 - (pg * PAGES_PER_STEP + i))], 0, 0)
        return pl.BlockSpec((None, None, PAGE * n_heads, dh), imap)

    page_specs = [page_spec(i) for i in range(PAGES_PER_STEP)]
    return pl.pallas_call(
        functools.partial(_sb_sample_kernel, n_heads=n_heads, dh=dh, n_groups=n_groups),
        grid_spec=pltpu.PrefetchScalarGridSpec(
            num_scalar_prefetch=1,
            grid=(n_seq, n_groups),
            in_specs=[pl.BlockSpec((None, n_heads, rows, dh), seq4),
                      pl.BlockSpec((None, n_heads, PAGE, dh), seq4),
                      pl.BlockSpec((None, n_heads, PAGE, dh), seq4),
                      pl.BlockSpec((1, n_heads * dh), lambda n, pg, pt: (0, 0)),
                      pl.BlockSpec((PAGE, PAGE), lambda n, pg, pt: (0, 0))] + page_specs + page_specs,
            out_specs=pl.BlockSpec((None, n_heads, rows, dh), seq4),
            scratch_shapes=[pltpu.VMEM((n_heads, rows, LANES), F32), pltpu.VMEM((n_heads, rows, dh), F32)],
        ),
        out_shape=jax.ShapeDtypeStruct((n_seq, n_heads, rows, dh), F32),
        compiler_params=_params("arbitrary", "arbitrary"),
        name="sb_sample",
    )(page_table.reshape(-1), qh, kn, vn, bias_lanes, u_tri,
      *([cache_k] * PAGES_PER_STEP), *([cache_v] * PAGES_PER_STEP))


def _sb_layer(x, xb, n_prompt, n_batch, dec_seq, w_qkv, bias, w_o, cache_k, cache_v, page_table, layer,
              ln_g, ln_b, alpha, n_heads, tm):
    t, d = x.shape
    dh = d // n_heads
    seq = n_prompt // n_batch
    n_seq = (t - n_prompt) // dec_seq
    q, k, v, kb, vb = _qkv(xb, w_qkv, 1.0 / math.sqrt(dh), tm)
    bias_lanes = jnp.repeat(bias.astype(F32), dh)[None, :]
    o_p = _sb_prompt(q, kb, vb, bias_lanes, n_batch, seq, dh, _pick_tile(seq, 256))

    def heads_first(a, pad_to):
        a = a[n_prompt:].reshape(n_seq, dec_seq, n_heads, dh).transpose(0, 2, 1, 3)
        return jnp.pad(a, ((0, 0), (0, 0), (0, pad_to - dec_seq), (0, 0)))

    o_s = _sb_sample(heads_first(q, 8), heads_first(kb, PAGE), heads_first(vb, PAGE), bias_lanes,
                     cache_k, cache_v, page_table, layer)
    o_s = o_s[:, :, :dec_seq].transpose(0, 2, 1, 3).reshape(t - n_prompt, d).astype(BF16)
    o_all = jnp.concatenate([o_p, o_s], axis=0)
    x1, x1b = _down_ln(o_all, w_o, x, ln_g, ln_b, alpha, tm)
    return x1, x1b, k, v


def kernel(x_prompt, x_sample, cache_k, cache_v, page_table, ln_g, ln_b, gm_w_uv, gm_vn_g, gm_vn_b, gm_w_s,
           gm_b_s, gm_w_o, sb_w_qkv, sb_bias, sb_w_o, ffn_w_gu, ffn_w_down, moe_w_router, moe_w_gu,
           moe_w_down):
    n_batch, seq, d = x_prompt.shape
    n_seq, dec_seq, _ = x_sample.shape
    depth = ln_g.shape[0]
    n_heads = sb_bias.shape[1]
    dh = d // n_heads
    groups = gm_w_s.shape[1]
    alpha = float((2 * depth) ** 0.25)
    n_prompt = n_batch * seq
    n_sample = n_seq * dec_seq
    t = n_prompt + n_sample
    tm = _pick_tile(math.gcd(n_prompt, n_sample), 512)

    x = jnp.concatenate([x_prompt.reshape(n_prompt, d), x_sample.reshape(n_sample, d)], axis=0)
    xb = None
    n_layers_kv, n_pool = cache_k.shape[0], cache_k.shape[1]
    ck = cache_k.reshape(n_layers_kv, n_pool, PAGE * n_heads, dh)
    cv = cache_v.reshape(n_layers_kv, n_pool, PAGE * n_heads, dh)
    vec = lambda a: a[None, :].astype(F32)

    tril = jnp.tril(jnp.ones((CHUNK, CHUNK), F32))
    idx = jnp.arange(CHUNK)
    same_seq = (idx[:, None] // dec_seq) == (idx[None, :] // dec_seq)

    k_list, v_list, gmv_list = [], [], []
    for i in range(depth):
        j = i // 2
        if i % 2 == 0:
            ws_p = gm_w_s[j] * tril
            ws_s = jnp.where(same_seq, (gm_w_s[j] * tril)[:, idx[:, None] % dec_seq, idx[None, :] % dec_seq], 0.0)
            ws2 = jnp.stack([ws_p, ws_s]).astype(BF16)
            gd = gm_w_o.shape[1] // groups
            bs_p = jnp.repeat(gm_b_s[j].T, gd, axis=1)
            bs_s = jnp.repeat(gm_b_s[j][:, idx % dec_seq].T, gd, axis=1)
            bs2 = jnp.stack([bs_p, bs_s])
            x, xb, vs = _gmlp_layer(x, n_prompt, gm_w_uv[j].astype(BF16), vec(gm_vn_g[j]), vec(gm_vn_b[j]),
                                    ws2, bs2, gm_w_o[j].astype(BF16), vec(ln_g[i, 0]), vec(ln_b[i, 0]),
                                    alpha, groups)
            gmv_list.append(vs.reshape(n_seq, dec_seq, -1))
            n_tiles = t // tm
            tid = jnp.arange(n_tiles, dtype=jnp.int32)
            hb = _swiglu_up(xb, ffn_w_gu[j][None].astype(BF16), jnp.zeros_like(tid), tid,
                            jnp.full((1,), n_tiles, jnp.int32), tm)
            x, xb = _down_ln(hb, ffn_w_down[j].astype(BF16), x, vec(ln_g[i, 1]), vec(ln_b[i, 1]), alpha, tm)
        else:
            x, xb, k, v = _sb_layer(x, xb, n_prompt, n_batch, dec_seq, sb_w_qkv[j].astype(BF16), sb_bias[j],
                                    sb_w_o[j].astype(BF16), ck, cv, page_table, j,
                                    vec(ln_g[i, 0]), vec(ln_b[i, 0]), alpha, n_heads, tm)
            k_list.append(k)
            v_list.append(v)
            x, xb = _moe_layer(x, xb, moe_w_router[j], moe_w_gu[j].astype(BF16), moe_w_down[j].astype(BF16),
                               vec(ln_g[i, 1]), vec(ln_b[i, 1]), alpha, tm, tm)

    ks, vs_ = jnp.stack(k_list), jnp.stack(v_list)
    n_kv = ks.shape[0]
    return (x[:n_prompt].reshape(n_batch, seq, d),
            x[n_prompt:].reshape(n_seq, dec_seq, d),
            ks[:, :n_prompt].reshape(n_kv, n_batch, seq, n_heads, dh),
            vs_[:, :n_prompt].reshape(n_kv, n_batch, seq, n_heads, dh),
            ks[:, n_prompt:].reshape(n_kv, n_seq, dec_seq, n_heads, dh),
            vs_[:, n_prompt:].reshape(n_kv, n_seq, dec_seq, n_heads, dh),
            jnp.stack(gmv_list))
```

```python
import functools
import math

import jax
import jax.numpy as jnp
from jax import lax
from jax.experimental import pallas as pl
from jax.experimental.pallas import tpu as pltpu

F32 = jnp.float32
BF16 = jnp.bfloat16
LN_EPS = 1e-5
CHUNK = 128
PAGE = 128
LANES = 128
V7X_VMEM_LIMIT_BYTES = 56 << 20
PAGES_PER_STEP = 4


def _params(*sem):
    return pltpu.CompilerParams(dimension_semantics=sem, vmem_limit_bytes=V7X_VMEM_LIMIT_BYTES)


def _ln_rows(x, g, b):
    mu = jnp.mean(x, axis=-1, keepdims=True)
    xc = x - mu
    var = jnp.mean(xc * xc, axis=-1, keepdims=True)
    return xc * lax.rsqrt(var + LN_EPS) * g + b


def _pick_tile(n, pref):
    t = pref
    while n % t:
        t //= 2
    return t


def _gmlp_kernel(x_ref, wuv_ref, vng_ref, vnb_ref, ws_ref, bs_ref, wo_ref, lng_ref, lnb_ref,
                 x1_ref, x1b_ref, vs_ref, y_ref, *, alpha, width, groups, n_prompt_tiles):
    tm = x_ref.shape[0]
    x = x_ref[...]
    h = jnp.dot(x.astype(BF16), wuv_ref[...], preferred_element_type=F32)
    h = 0.5 * h * (1.0 + lax.erf(h * (1.0 / math.sqrt(2.0))))
    u = h[:, :width]
    v = _ln_rows(h[:, width:], vng_ref[...], vnb_ref[...])

    @pl.when(pl.program_id(0) >= n_prompt_tiles)
    def _():
        vs_ref[...] = v

    vb = v.astype(BF16)
    gd = width // groups
    for c in range(tm // CHUNK):
        r0 = c * CHUNK
        for g in range(groups):
            c0 = g * gd
            s = jnp.dot(ws_ref[g], vb[r0:r0 + CHUNK, c0:c0 + gd], preferred_element_type=F32)
            s = s + bs_ref[:, c0:c0 + gd]
            y_ref[r0:r0 + CHUNK, c0:c0 + gd] = (u[r0:r0 + CHUNK, c0:c0 + gd] * s).astype(BF16)
    m = jnp.dot(y_ref[...], wo_ref[...], preferred_element_type=F32)
    x1 = _ln_rows(alpha * x + m, lng_ref[...], lnb_ref[...])
    x1_ref[...] = x1
    x1b_ref[...] = x1.astype(BF16)


def _gmlp_layer(x, n_prompt, w_uv, vn_g, vn_b, ws2, bs2, w_o, ln_g, ln_b, alpha, groups):
    t, d = x.shape
    width = w_o.shape[0]
    tm = 2 * CHUNK
    npt = n_prompt // tm
    nst = (t - n_prompt) // tm
    row = lambda i: (i, 0)
    const2 = lambda i: (0, 0)
    variant = lambda i: (jnp.where(i >= npt, 1, 0), 0, 0, 0)
    variant3 = lambda i: (jnp.where(i >= npt, 1, 0), 0, 0)
    kern = functools.partial(_gmlp_kernel, alpha=alpha, width=width, groups=groups, n_prompt_tiles=npt)
    return pl.pallas_call(
        kern,
        grid=(npt + nst,),
        in_specs=[
            pl.BlockSpec((tm, d), row),
            pl.BlockSpec((d, 2 * width), const2),
            pl.BlockSpec((1, width), const2),
            pl.BlockSpec((1, width), const2),
            pl.BlockSpec((None, groups, CHUNK, CHUNK), variant),
            pl.BlockSpec((None, CHUNK, width), variant3),
            pl.BlockSpec((width, d), const2),
            pl.BlockSpec((1, d), const2),
            pl.BlockSpec((1, d), const2),
        ],
        out_specs=[
            pl.BlockSpec((tm, d), row),
            pl.BlockSpec((tm, d), row),
            pl.BlockSpec((tm, width), lambda i: (jnp.maximum(i - npt, 0), 0)),
        ],
        out_shape=[
            jax.ShapeDtypeStruct((t, d), F32),
            jax.ShapeDtypeStruct((t, d), BF16),
            jax.ShapeDtypeStruct((t - n_prompt, width), F32),
        ],
        scratch_shapes=[pltpu.VMEM((tm, width), BF16)],
        compiler_params=_params("arbitrary"),
        name="gmlp_layer",
    )(x, w_uv, vn_g, vn_b, ws2, bs2, w_o, ln_g, ln_b)


def _swiglu_up_kernel(te_ref, ts_ref, nu_ref, x_ref, wg_ref, wu_ref, h_ref):
    @pl.when(pl.program_id(1) < nu_ref[0])
    def _():
        x = x_ref[...]
        g = jnp.dot(x, wg_ref[...], preferred_element_type=F32)
        u = jnp.dot(x, wu_ref[...], preferred_element_type=F32)
        h_ref[...] = (g * (1.0 / (1.0 + jnp.exp(-g))) * u).astype(BF16)


def _swiglu_up(xb, w_gu, tile_expert, tile_src, n_used, tm):
    d = xb.shape[1]
    f = w_gu.shape[2] // 2
    tf = f // 2
    nf = f // tf
    n_tiles = tile_expert.shape[0]
    return pl.pallas_call(
        _swiglu_up_kernel,
        grid_spec=pltpu.PrefetchScalarGridSpec(
            num_scalar_prefetch=3,
            grid=(nf, n_tiles),
            in_specs=[
                pl.BlockSpec((tm, d), lambda j, t, te, ts, nu: (ts[t], 0)),
                pl.BlockSpec((None, d, tf), lambda j, t, te, ts, nu: (te[t], 0, j)),
                pl.BlockSpec((None, d, tf), lambda j, t, te, ts, nu: (te[t], 0, nf + j)),
            ],
            out_specs=pl.BlockSpec((tm, tf), lambda j, t, te, ts, nu: (jnp.minimum(t, nu[0] - 1), j)),
        ),
        out_shape=jax.ShapeDtypeStruct((n_tiles * tm, f), BF16),
        compiler_params=_params("arbitrary", "arbitrary"),
        name="swiglu_up",
    )(tile_expert, tile_src, n_used, xb, w_gu, w_gu)


def _down_ln_kernel(h_ref, wd_ref, xres_ref, lng_ref, lnb_ref, x2_ref, x2b_ref, *, alpha):
    m = jnp.dot(h_ref[...], wd_ref[...], preferred_element_type=F32)
    x2 = _ln_rows(alpha * xres_ref[...] + m, lng_ref[...], lnb_ref[...])
    x2_ref[...] = x2
    x2b_ref[...] = x2.astype(BF16)


def _down_ln(hb, w_down, xres, ln_g, ln_b, alpha, tm):
    t, f = hb.shape
    d = w_down.shape[1]
    row = lambda i: (i, 0)
    const2 = lambda i: (0, 0)
    return pl.pallas_call(
        functools.partial(_down_ln_kernel, alpha=alpha),
        grid=(t // tm,),
        in_specs=[
            pl.BlockSpec((tm, f), row),
            pl.BlockSpec((f, d), const2),
            pl.BlockSpec((tm, d), row),
            pl.BlockSpec((1, d), const2),
            pl.BlockSpec((1, d), const2),
        ],
        out_specs=[pl.BlockSpec((tm, d), row), pl.BlockSpec((tm, d), row)],
        out_shape=[jax.ShapeDtypeStruct((t, d), F32), jax.ShapeDtypeStruct((t, d), BF16)],
        compiler_params=_params("arbitrary"),
        name="down_ln",
    )(hb, w_down, xres, ln_g, ln_b)


def _down_gate_kernel(te_ref, nu_ref, h_ref, wd_ref, gate_ref, y_ref):
    @pl.when(pl.program_id(0) < nu_ref[0])
    def _():
        m = jnp.dot(h_ref[...], wd_ref[...], preferred_element_type=F32)
        y_ref[...] = (m * gate_ref[:, 0:1]).astype(BF16)


def _down_gate(hg, w_down, gate_rows, tile_expert, n_used, tm):
    f, d = w_down.shape[1], w_down.shape[2]
    n_tiles = tile_expert.shape[0]
    clamp = lambda t, te, nu: (jnp.minimum(t, nu[0] - 1), 0)
    return pl.pallas_call(
        _down_gate_kernel,
        grid_spec=pltpu.PrefetchScalarGridSpec(
            num_scalar_prefetch=2,
            grid=(n_tiles,),
            in_specs=[
                pl.BlockSpec((tm, f), clamp),
                pl.BlockSpec((None, f, d), lambda t, te, nu: (te[t], 0, 0)),
                pl.BlockSpec((tm, LANES), clamp),
            ],
            out_specs=pl.BlockSpec((tm, d), clamp),
        ),
        out_shape=jax.ShapeDtypeStruct((n_tiles * tm, d), BF16),
        compiler_params=_params("arbitrary"),
        name="down_gate",
    )(tile_expert, n_used, hg, w_down, gate_rows)


def _split_bf16(a):
    hi = a.astype(BF16)
    lo = (a - hi.astype(F32)).astype(BF16)
    return hi, lo


def _router_kernel(x_ref, wr_ref, tri_ref, info_ref, base_ref, cnt_ref, *, n_experts):
    i = pl.program_id(0)

    @pl.when(i == 0)
    def _():
        cnt_ref[...] = jnp.zeros_like(cnt_ref)

    xh, xl = _split_bf16(x_ref[...])
    wh, wl = _split_bf16(wr_ref[...])
    logits = (jnp.dot(xh, wh, preferred_element_type=F32) + jnp.dot(xl, wh, preferred_element_type=F32)
              + jnp.dot(xh, wl, preferred_element_type=F32))
    tm = logits.shape[0]
    lane = lax.broadcasted_iota(jnp.int32, (tm, LANES), 1).astype(F32)
    neg = jnp.float32(-jnp.inf)
    lg = jnp.where(lane < n_experts, logits, neg)
    m1 = jnp.max(lg, axis=1, keepdims=True)
    i1 = jnp.min(jnp.where(lg == m1, lane, float(LANES)), axis=1, keepdims=True)
    lg2 = jnp.where(lane == i1, neg, lg)
    m2 = jnp.max(lg2, axis=1, keepdims=True)
    i2 = jnp.min(jnp.where(lg2 == m2, lane, float(LANES)), axis=1, keepdims=True)
    dd = jnp.exp(m2 - m1)
    g1 = 1.0 / (1.0 + dd)
    g2 = dd / (1.0 + dd)
    oh1 = lane == i1
    oh2 = lane == i2
    both = jnp.where(oh1 | oh2, 1.0, 0.0)
    prefix = jnp.dot(tri_ref[...], both.astype(BF16), preferred_element_type=F32)
    base = cnt_ref[0:1, :]
    rank_all = prefix + base
    r1 = jnp.sum(jnp.where(oh1, rank_all, 0.0), axis=1, keepdims=True)
    r2 = jnp.sum(jnp.where(oh2, rank_all, 0.0), axis=1, keepdims=True)
    info = jnp.where(lane == 0, i1, 0.0)
    info = jnp.where(lane == 1, i2, info)
    info = jnp.where(lane == 2, g1, info)
    info = jnp.where(lane == 3, g2, info)
    info = jnp.where(lane == 4, r1, info)
    info = jnp.where(lane == 5, r2, info)
    info_ref[...] = info
    base_ref[...] = base
    cnt_ref[0:1, :] = base + jnp.sum(both, axis=0, keepdims=True)


def _router(x1, w_router, tc):
    t, d = x1.shape
    n_experts = w_router.shape[1]
    wr = jnp.pad(w_router, ((0, 0), (0, LANES - n_experts)))
    tri = (jnp.arange(tc)[:, None] > jnp.arange(tc)[None, :]).astype(BF16)
    n_chunks = t // tc
    info, base, cnt = pl.pallas_call(
        functools.partial(_router_kernel, n_experts=n_experts),
        grid=(n_chunks,),
        in_specs=[
            pl.BlockSpec((tc, d), lambda i: (i, 0)),
            pl.BlockSpec((d, LANES), lambda i: (0, 0)),
            pl.BlockSpec((tc, tc), lambda i: (0, 0)),
        ],
        out_specs=[
            pl.BlockSpec((tc, LANES), lambda i: (i, 0)),
            pl.BlockSpec((None, 1, LANES), lambda i: (i, 0, 0)),
            pl.BlockSpec((8, LANES), lambda i: (0, 0)),
        ],
        out_shape=[
            jax.ShapeDtypeStruct((t, LANES), F32),
            jax.ShapeDtypeStruct((n_chunks, 1, LANES), F32),
            jax.ShapeDtypeStruct((8, LANES), F32),
        ],
        compiler_params=_params("arbitrary"),
        name="router",
    )(x1, wr, tri)
    return info, base[:, 0, :n_experts].astype(jnp.int32), cnt[0, :n_experts].astype(jnp.int32)


def _dispatch_kernel(vt_ref, vc_ref, vf_ref, vok_ref, x_ref, p1_ref, p2_ref, g1_ref, g2_ref,
                     xg_ref, gg_ref):
    v = pl.program_id(0)
    tmg = xg_ref.shape[0]

    @pl.when(vf_ref[v] == 1)
    def _():
        xg_ref[...] = jnp.zeros_like(xg_ref)
        gg_ref[...] = jnp.zeros_like(gg_ref)

    @pl.when(vok_ref[v] == 1)
    def _():
        tc = x_ref.shape[0]
        rows = vt_ref[v] * tmg + lax.broadcasted_iota(jnp.int32, (tmg, tc), 0)
        sel1 = p1_ref[...] == rows
        sel2 = p2_ref[...] == rows
        onehot = jnp.where(sel1 | sel2, 1.0, 0.0).astype(BF16)
        xg_ref[...] += jnp.dot(onehot, x_ref[...], preferred_element_type=F32).astype(BF16)
        gate = jnp.where(sel1, g1_ref[...], 0.0) + jnp.where(sel2, g2_ref[...], 0.0)
        gg_ref[...] += jnp.broadcast_to(jnp.sum(gate, axis=1, keepdims=True), gg_ref.shape)


def _dispatch(xb, pos1, pos2, g1, g2, visits, n_tiles, tmg, tc):
    t, d = xb.shape
    n_chunks = t // tc
    vt, vc, vf, vok = visits
    as_rows = lambda a: a.reshape(n_chunks, 1, tc)
    chunk3 = pl.BlockSpec((None, 1, tc), lambda v, vt, vc, vf, vok: (vc[v], 0, 0))
    tile = lambda v, vt, vc, vf, vok: (vt[v], 0)
    return pl.pallas_call(
        _dispatch_kernel,
        grid_spec=pltpu.PrefetchScalarGridSpec(
            num_scalar_prefetch=4,
            grid=(vt.shape[0],),
            in_specs=[pl.BlockSpec((tc, d), lambda v, vt, vc, vf, vok: (vc[v], 0)),
                      chunk3, chunk3, chunk3, chunk3],
            out_specs=[pl.BlockSpec((tmg, d), tile), pl.BlockSpec((tmg, LANES), tile)],
        ),
        out_shape=[jax.ShapeDtypeStruct((n_tiles * tmg, d), BF16),
                   jax.ShapeDtypeStruct((n_tiles * tmg, LANES), F32)],
        compiler_params=_params("arbitrary"),
        name="moe_dispatch",
    )(vt, vc, vf, vok, xb, as_rows(pos1), as_rows(pos2), as_rows(g1), as_rows(g2))


def _combine_kernel(vt_ref, vc_ref, vf_ref, vl_ref, vok_ref, yg_ref, p1_ref, p2_ref, xres_ref,
                    lng_ref, lnb_ref, x2_ref, x2b_ref, acc_ref, *, alpha):
    v = pl.program_id(0)
    tmg = yg_ref.shape[0]

    @pl.when(vf_ref[v] == 1)
    def _():
        acc_ref[...] = jnp.zeros_like(acc_ref)

    @pl.when(vok_ref[v] == 1)
    def _():
        tc = acc_ref.shape[0]
        rows = vt_ref[v] * tmg + lax.broadcasted_iota(jnp.int32, (tc, tmg), 1)
        onehot = jnp.where((p1_ref[...] == rows) | (p2_ref[...] == rows), 1.0, 0.0).astype(BF16)
        acc_ref[...] += jnp.dot(onehot, yg_ref[...], preferred_element_type=F32)

    @pl.when(vl_ref[v] == 1)
    def _():
        x2 = _ln_rows(alpha * xres_ref[...] + acc_ref[...], lng_ref[...], lnb_ref[...])
        x2_ref[...] = x2
        x2b_ref[...] = x2.astype(BF16)


def _combine(yg, pos1, pos2, xres, ln_g, ln_b, visits, alpha, tmg, tc):
    t, d = xres.shape
    vt, vc, vf, vl, vok = visits
    chunk = lambda v, vt, vc, vf, vl, vok: (vc[v], 0)
    const2 = lambda v, vt, vc, vf, vl, vok: (0, 0)
    return pl.pallas_call(
        functools.partial(_combine_kernel, alpha=alpha),
        grid_spec=pltpu.PrefetchScalarGridSpec(
            num_scalar_prefetch=5,
            grid=(vt.shape[0],),
            in_specs=[
                pl.BlockSpec((tmg, d), lambda v, vt, vc, vf, vl, vok: (vt[v], 0)),
                pl.BlockSpec((tc, 1), chunk),
                pl.BlockSpec((tc, 1), chunk),
                pl.BlockSpec((tc, d), chunk),
                pl.BlockSpec((1, d), const2),
                pl.BlockSpec((1, d), const2),
            ],
            out_specs=[pl.BlockSpec((tc, d), chunk), pl.BlockSpec((tc, d), chunk)],
            scratch_shapes=[pltpu.VMEM((tc, d), F32)],
        ),
        out_shape=[jax.ShapeDtypeStruct((t, d), F32), jax.ShapeDtypeStruct((t, d), BF16)],
        compiler_params=_params("arbitrary"),
        name="moe_combine",
    )(vt, vc, vf, vl, vok, yg, pos1.reshape(t, 1), pos2.reshape(t, 1), xres, ln_g, ln_b)


def _expand_visits(lo, hi, active, n_max):
    n = jnp.where(active, hi - lo + 1, 0)
    end = jnp.cumsum(n)
    start = end - n
    total = end[-1]
    v = jnp.arange(n_max, dtype=jnp.int32)
    ok = v < total
    vv = jnp.minimum(v, total - 1)
    owner = jnp.searchsorted(end, vv, side="right").astype(jnp.int32)
    idx = lo[owner] + (vv - start[owner])
    first = ok & (vv == start[owner])
    last = ok & (vv == end[owner] - 1)
    i32 = lambda a: a.astype(jnp.int32)
    return owner, i32(idx), i32(first), i32(last), i32(ok)


def _moe_layer(x1, x1b, w_router, w_gu, w_down, ln_g, ln_b, alpha, tmg, tc):
    t, d = x1.shape
    n_experts = w_router.shape[1]
    n_chunks = t // tc
    info, base, counts = _router(x1, w_router, tc)
    e1 = info[:, 0].astype(jnp.int32)
    e2 = info[:, 1].astype(jnp.int32)
    g1, g2 = info[:, 2], info[:, 3]
    r1 = info[:, 4].astype(jnp.int32)
    r2 = info[:, 5].astype(jnp.int32)

    n_tiles = (2 * t) // tmg + n_experts
    tiles_e = (counts + tmg - 1) // tmg
    tile_end = jnp.cumsum(tiles_e)
    tile_start = tile_end - tiles_e
    n_used = tile_end[-1]
    off = tile_start * tmg
    pos1 = (off[e1] + r1).astype(jnp.int32)
    pos2 = (off[e2] + r2).astype(jnp.int32)
    tid = jnp.arange(n_tiles, dtype=jnp.int32)
    tile_expert = jnp.minimum(jnp.searchsorted(tile_end, jnp.minimum(tid, n_used - 1), side="right"),
                              n_experts - 1).astype(jnp.int32)

    cum_incl = jnp.concatenate([base[1:], counts[None, :]], axis=0)
    k = tid - tile_start[tile_expert]
    lo_rank = k * tmg
    hi_rank = jnp.minimum(lo_rank + tmg, counts[tile_expert]) - 1
    cum_t = cum_incl[:, tile_expert]
    c_lo = jnp.sum(cum_t <= lo_rank[None, :], axis=0).astype(jnp.int32)
    c_hi = jnp.sum(cum_t <= hi_rank[None, :], axis=0).astype(jnp.int32)
    n_visits = n_experts * n_chunks + n_tiles
    d_owner, d_idx, d_first, _, d_ok = _expand_visits(c_lo, c_hi, tid < n_used, n_visits)
    xg, gate_rows = _dispatch(x1b, pos1, pos2, g1, g2, (d_owner, d_idx, d_first, d_ok), n_tiles, tmg, tc)

    n_used_arr = n_used.reshape(1).astype(jnp.int32)
    hg = _swiglu_up(xg, w_gu, tile_expert, jnp.minimum(tid, n_used - 1), n_used_arr, tmg)
    yg = _down_gate(hg, w_down, gate_rows, tile_expert, n_used_arr, tmg)

    cnt_ce = cum_incl - base
    t_lo = (off[None, :] + base) // tmg
    t_hi = (off[None, :] + cum_incl - 1) // tmg
    c_owner, c_idx, c_first, c_last, c_ok = _expand_visits(
        t_lo.reshape(-1).astype(jnp.int32), t_hi.reshape(-1).astype(jnp.int32),
        (cnt_ce > 0).reshape(-1), n_visits)
    chunk_of = c_owner // n_experts
    prev_chunk = jnp.concatenate([jnp.full((1,), -1, jnp.int32), chunk_of[:-1]])
    next_chunk = jnp.concatenate([chunk_of[1:], jnp.full((1,), -1, jnp.int32)])
    next_ok = jnp.concatenate([c_ok[1:], jnp.zeros((1,), jnp.int32)])
    first = (c_ok == 1) & (chunk_of != prev_chunk)
    last = (c_ok == 1) & ((chunk_of != next_chunk) | (next_ok == 0))
    visits = (c_idx, chunk_of.astype(jnp.int32), first.astype(jnp.int32), last.astype(jnp.int32), c_ok)
    return _combine(yg, pos1, pos2, x1, ln_g, ln_b, visits, alpha, tmg, tc)


def _qkv_kernel(x_ref, w_ref, q_ref, k_ref, v_ref, kb_ref, vb_ref, *, d, scale):
    acc = jnp.dot(x_ref[...], w_ref[...], preferred_element_type=F32)
    k = acc[:, d:2 * d]
    v = acc[:, 2 * d:]
    q_ref[...] = (acc[:, :d] * scale).astype(BF16)
    k_ref[...] = k
    v_ref[...] = v
    kb_ref[...] = k.astype(BF16)
    vb_ref[...] = v.astype(BF16)


def _qkv(xb, w_qkv, scale, tm):
    t, d = xb.shape
    row = lambda i: (i, 0)
    spec = pl.BlockSpec((tm, d), row)
    return pl.pallas_call(
        functools.partial(_qkv_kernel, d=d, scale=scale),
        grid=(t // tm,),
        in_specs=[spec, pl.BlockSpec((d, 3 * d), lambda i: (0, 0))],
        out_specs=[spec] * 5,
        out_shape=[jax.ShapeDtypeStruct((t, d), BF16), jax.ShapeDtypeStruct((t, d), F32),
                   jax.ShapeDtypeStruct((t, d), F32), jax.ShapeDtypeStruct((t, d), BF16),
                   jax.ShapeDtypeStruct((t, d), BF16)],
        compiler_params=_params("arbitrary"),
        name="qkv",
    )(xb, w_qkv)


def _sb_block(z, mask, carry, acc, v_blk, u_tri):
    lg = jnp.log(1.0 + jnp.exp(-jnp.abs(z)))
    sp = jnp.maximum(z, 0.0) + lg
    arg = jnp.minimum(z, 0.0) - lg
    if mask is not None:
        sp = jnp.where(mask, sp, 0.0)
    tail = jnp.dot(sp.astype(BF16), u_tri, preferred_element_type=F32)
    a = jnp.exp(arg - tail - carry)
    if mask is not None:
        a = jnp.where(mask, a, 0.0)
    acc = acc + jnp.dot(a.astype(BF16), v_blk, preferred_element_type=F32)
    carry = carry + jnp.sum(sp, axis=1, keepdims=True)
    return carry, acc


def _sb_prompt_kernel(q_ref, k_ref, v_ref, b_ref, u_ref, o_ref, *, tq, dh):
    s = q_ref.shape[0]
    lane = lax.broadcasted_iota(jnp.int32, (1, LANES), 1)
    diag = lax.broadcasted_iota(jnp.int32, (tq, tq), 1) < lax.broadcasted_iota(jnp.int32, (tq, tq), 0)
    u_tri = u_ref[...]
    nt = (((1,), (1,)), ((), ()))

    def q_body(qi, _):
        qoff = pl.multiple_of(qi * tq, tq)
        q2 = q_ref[pl.ds(qoff, tq), :]
        outs = []
        for hh in range(LANES // dh):
            in_head = (lane >= hh * dh) & (lane < (hh + 1) * dh)
            qh = jnp.where(in_head, q2, jnp.zeros_like(q2))
            bh = b_ref[:, hh * dh:hh * dh + 1]

            def blk(kb, mask, carry, acc):
                koff = pl.multiple_of(kb * tq, tq)
                z = lax.dot_general(qh, k_ref[pl.ds(koff, tq), :], nt, preferred_element_type=F32) + bh
                return _sb_block(z, mask, carry, acc, v_ref[pl.ds(koff, tq), :], u_tri)

            carry, acc = blk(qi, diag, jnp.zeros((tq, 1), F32), jnp.zeros((tq, LANES), F32))
            carry, acc = lax.fori_loop(0, qi, lambda i, ca: blk(qi - 1 - i, None, *ca), (carry, acc))
            outs.append(acc)
        o = outs[0]
        for hh in range(1, len(outs)):
            o = jnp.where(lane >= hh * dh, outs[hh], o)
        o_ref[pl.ds(qoff, tq), :] = o.astype(BF16)
        return 0

    lax.fori_loop(0, s // tq, q_body, 0)


def _sb_prompt(q, kb, vb, bias_lanes, n_batch, seq, dh, tq):
    d = q.shape[1]
    u_tri = (jnp.arange(tq)[:, None] > jnp.arange(tq)[None, :]).astype(BF16)
    blk = pl.BlockSpec((seq, LANES), lambda n, hp: (n, hp))
    return pl.pallas_call(
        functools.partial(_sb_prompt_kernel, tq=tq, dh=dh),
        grid=(n_batch, d // LANES),
        in_specs=[blk, blk, blk,
                  pl.BlockSpec((1, LANES), lambda n, hp: (0, hp)),
                  pl.BlockSpec((tq, tq), lambda n, hp: (0, 0))],
        out_specs=blk,
        out_shape=jax.ShapeDtypeStruct((n_batch * seq, d), BF16),
        compiler_params=_params("arbitrary", "arbitrary"),
        name="sb_prompt",
    )(q, kb, vb, bias_lanes, u_tri)


def _sb_sample_kernel(pt_ref, q_ref, kn_ref, vn_ref, b_ref, u_ref, *refs, n_heads, dh, n_groups):
    np_ = PAGES_PER_STEP
    kp_refs, vp_refs = refs[:np_], refs[np_:2 * np_]
    o_ref, carry_ref, acc_ref = refs[2 * np_:2 * np_ + 3]
    kflat_refs, vflat_refs = refs[2 * np_ + 3:3 * np_ + 3], refs[3 * np_ + 3:]
    pg = pl.program_id(1)
    u_tri = u_ref[...]
    nt = (((1,), (1,)), ((), ()))
    rows = q_ref.shape[1]
    pitch = kflat_refs[0].shape[0] // PAGE
    bias_rows = b_ref[...]

    def sweep(k_of, v_of, mask):
        z = jnp.concatenate([lax.dot_general(q_ref[h], k_of(h), nt, preferred_element_type=F32)
                             for h in range(n_heads)], axis=0) + bias_rows
        lg = jnp.log(1.0 + jnp.exp(-jnp.abs(z)))
        sp = jnp.maximum(z, 0.0) + lg
        arg = jnp.minimum(z, 0.0) - lg
        if mask is not None:
            sp = jnp.where(mask, sp, 0.0)
        tail = jnp.dot(sp.astype(BF16), u_tri, preferred_element_type=F32)
        a = jnp.exp(arg - tail - carry_ref[:, 0:1])
        if mask is not None:
            a = jnp.where(mask, a, 0.0)
        for h in range(n_heads):
            a_h = a[h * rows:(h + 1) * rows].astype(BF16)
            acc_ref[h] += jnp.dot(a_h, v_of(h), preferred_element_type=F32)
        carry_ref[...] += jnp.broadcast_to(jnp.sum(sp, axis=1, keepdims=True), carry_ref.shape)

    @pl.when(pg == 0)
    def _():
        carry_ref[...] = jnp.zeros_like(carry_ref)
        acc_ref[...] = jnp.zeros_like(acc_ref)
        shape = (n_heads * rows, PAGE)
        query = lax.broadcasted_iota(jnp.int32, shape, 0) & (rows - 1)
        mask = lax.broadcasted_iota(jnp.int32, shape, 1) < query
        sweep(lambda h: kn_ref[h], lambda h: vn_ref[h], mask)

    for i in range(np_):
        for key in range(PAGE):
            kflat_refs[i][key * pitch:key * pitch + n_heads, :] = kp_refs[i][key]
            vflat_refs[i][key * pitch:key * pitch + n_heads, :] = vp_refs[i][key]
        sweep(lambda h: kflat_refs[i][pl.ds(h, PAGE, stride=pitch), :].astype(BF16),
              lambda h: vflat_refs[i][pl.ds(h, PAGE, stride=pitch), :].astype(BF16), None)

    @pl.when(pg == n_groups - 1)
    def _():
        o_ref[...] = acc_ref[...]


def _sb_sample(qh, kn, vn, bias, cache_k, cache_v, page_table, layer):
    n_seq, n_heads, rows, dh = qh.shape
    assert rows & (rows - 1) == 0
    n_pages = page_table.shape[1]
    n_groups = n_pages // PAGES_PER_STEP
    u_tri = (jnp.arange(PAGE)[:, None] > jnp.arange(PAGE)[None, :]).astype(BF16)
    seq4 = lambda n, pg, pt: (n, 0, 0, 0)

    def page_spec(i):
        def imap(n, pg, pt):
            return (layer, pt[n * n_pages + (n_pages - 1 - (pg * PAGES_PER_STEP + i))], 0, 0, 0)
        return pl.BlockSpec((None, None, PAGE, n_heads, dh), imap)

    page_specs = [page_spec(i) for i in range(PAGES_PER_STEP)]
    return pl.pallas_call(
        functools.partial(_sb_sample_kernel, n_heads=n_heads, dh=dh, n_groups=n_groups),
        grid_spec=pltpu.PrefetchScalarGridSpec(
            num_scalar_prefetch=1,
            grid=(n_seq, n_groups),
            in_specs=[pl.BlockSpec((None, n_heads, rows, dh), seq4),
                      pl.BlockSpec((None, n_heads, PAGE, dh), seq4),
                      pl.BlockSpec((None, n_heads, PAGE, dh), seq4),
                      pl.BlockSpec((n_heads * rows, 1), lambda n, pg, pt: (0, 0)),
                      pl.BlockSpec((PAGE, PAGE), lambda n, pg, pt: (0, 0))] + page_specs + page_specs,
            out_specs=pl.BlockSpec((None, n_heads, rows, dh), seq4),
            scratch_shapes=[pltpu.VMEM((n_heads * rows, LANES), F32), pltpu.VMEM((n_heads, rows, dh), F32)]
            + [pltpu.VMEM((PAGE * (n_heads + 8), dh), F32)] * (2 * PAGES_PER_STEP),
        ),
        out_shape=jax.ShapeDtypeStruct((n_seq, n_heads, rows, dh), F32),
        compiler_params=_params("arbitrary", "arbitrary"),
        name="sb_sample",
    )(page_table.reshape(-1), qh, kn, vn, jnp.repeat(bias.astype(F32), rows)[:, None], u_tri,
      *([cache_k] * PAGES_PER_STEP), *([cache_v] * PAGES_PER_STEP))


def _sb_layer(x, xb, n_prompt, n_batch, dec_seq, w_qkv, bias, w_o, cache_k, cache_v, page_table, layer,
              ln_g, ln_b, alpha, n_heads, tm):
    t, d = x.shape
    dh = d // n_heads
    seq = n_prompt // n_batch
    n_seq = (t - n_prompt) // dec_seq
    q, k, v, kb, vb = _qkv(xb, w_qkv, 1.0 / math.sqrt(dh), tm)
    bias_lanes = jnp.repeat(bias.astype(F32), dh)[None, :]
    o_p = _sb_prompt(q, kb, vb, bias_lanes, n_batch, seq, dh, _pick_tile(seq, 256))

    def heads_first(a, pad_to):
        a = a[n_prompt:].reshape(n_seq, dec_seq, n_heads, dh).transpose(0, 2, 1, 3)
        return jnp.pad(a, ((0, 0), (0, 0), (0, pad_to - dec_seq), (0, 0)))

    o_s = _sb_sample(heads_first(q, 8), heads_first(kb, PAGE), heads_first(vb, PAGE), bias,
                     cache_k, cache_v, page_table, layer)
    o_s = o_s[:, :, :dec_seq].transpose(0, 2, 1, 3).reshape(t - n_prompt, d).astype(BF16)
    o_all = jnp.concatenate([o_p, o_s], axis=0)
    x1, x1b = _down_ln(o_all, w_o, x, ln_g, ln_b, alpha, tm)
    return x1, x1b, k, v


def kernel(x_prompt, x_sample, cache_k, cache_v, page_table, ln_g, ln_b, gm_w_uv, gm_vn_g, gm_vn_b, gm_w_s,
           gm_b_s, gm_w_o, sb_w_qkv, sb_bias, sb_w_o, ffn_w_gu, ffn_w_down, moe_w_router, moe_w_gu,
           moe_w_down):
    n_batch, seq, d = x_prompt.shape
    n_seq, dec_seq, _ = x_sample.shape
    depth = ln_g.shape[0]
    n_heads = sb_bias.shape[1]
    dh = d // n_heads
    groups = gm_w_s.shape[1]
    alpha = float((2 * depth) ** 0.25)
    n_prompt = n_batch * seq
    n_sample = n_seq * dec_seq
    t = n_prompt + n_sample
    tm = _pick_tile(math.gcd(n_prompt, n_sample), 512)

    x = jnp.concatenate([x_prompt.reshape(n_prompt, d), x_sample.reshape(n_sample, d)], axis=0)
    xb = None
    vec = lambda a: a[None, :].astype(F32)

    tril = jnp.tril(jnp.ones((CHUNK, CHUNK), F32))
    idx = jnp.arange(CHUNK)
    same_seq = (idx[:, None] // dec_seq) == (idx[None, :] // dec_seq)

    k_list, v_list, gmv_list = [], [], []
    for i in range(depth):
        j = i // 2
        if i % 2 == 0:
            ws_p = gm_w_s[j] * tril
            ws_s = jnp.where(same_seq, (gm_w_s[j] * tril)[:, idx[:, None] % dec_seq, idx[None, :] % dec_seq], 0.0)
            ws2 = jnp.stack([ws_p, ws_s]).astype(BF16)
            gd = gm_w_o.shape[1] // groups
            bs_p = jnp.repeat(gm_b_s[j].T, gd, axis=1)
            bs_s = jnp.repeat(gm_b_s[j][:, idx % dec_seq].T, gd, axis=1)
            bs2 = jnp.stack([bs_p, bs_s])
            x, xb, vs = _gmlp_layer(x, n_prompt, gm_w_uv[j].astype(BF16), vec(gm_vn_g[j]), vec(gm_vn_b[j]),
                                    ws2, bs2, gm_w_o[j].astype(BF16), vec(ln_g[i, 0]), vec(ln_b[i, 0]),
                                    alpha, groups)
            gmv_list.append(vs.reshape(n_seq, dec_seq, -1))
            n_tiles = t // tm
            tid = jnp.arange(n_tiles, dtype=jnp.int32)
            hb = _swiglu_up(xb, ffn_w_gu[j][None].astype(BF16), jnp.zeros_like(tid), tid,
                            jnp.full((1,), n_tiles, jnp.int32), tm)
            x, xb = _down_ln(hb, ffn_w_down[j].astype(BF16), x, vec(ln_g[i, 1]), vec(ln_b[i, 1]), alpha, tm)
        else:
            x, xb, k, v = _sb_layer(x, xb, n_prompt, n_batch, dec_seq, sb_w_qkv[j].astype(BF16), sb_bias[j],
                                    sb_w_o[j].astype(BF16), cache_k, cache_v, page_table, j,
                                    vec(ln_g[i, 0]), vec(ln_b[i, 0]), alpha, n_heads, tm)
            k_list.append(k)
            v_list.append(v)
            x, xb = _moe_layer(x, xb, moe_w_router[j], moe_w_gu[j].astype(BF16), moe_w_down[j].astype(BF16),
                               vec(ln_g[i, 1]), vec(ln_b[i, 1]), alpha, tm, tm)

    ks, vs_ = jnp.stack(k_list), jnp.stack(v_list)
    n_kv = ks.shape[0]
    return (x[:n_prompt].reshape(n_batch, seq, d),
            x[n_prompt:].reshape(n_seq, dec_seq, d),
            ks[:, :n_prompt].reshape(n_kv, n_batch, seq, n_heads, dh),
            vs_[:, :n_prompt].reshape(n_kv, n_batch, seq, n_heads, dh),
            ks[:, n_prompt:].reshape(n_kv, n_seq, dec_seq, n_heads, dh),
            vs_[:, n_prompt:].reshape(n_kv, n_seq, dec_seq, n_heads, dh),
            jnp.stack(gmv_list))
```

```python
import functools
import math

import jax
import jax.numpy as jnp
from jax import lax
from jax.experimental import pallas as pl
from jax.experimental.pallas import tpu as pltpu

F32 = jnp.float32
BF16 = jnp.bfloat16
LN_EPS = 1e-5
CHUNK = 128
PAGE = 128
LANES = 128
V7X_VMEM_LIMIT_BYTES = 56 << 20
PAGES_PER_STEP = 4


def _params(*sem):
    return pltpu.CompilerParams(dimension_semantics=sem, vmem_limit_bytes=V7X_VMEM_LIMIT_BYTES)


def _ln_rows(x, g, b):
    mu = jnp.mean(x, axis=-1, keepdims=True)
    xc = x - mu
    var = jnp.mean(xc * xc, axis=-1, keepdims=True)
    return xc * lax.rsqrt(var + LN_EPS) * g + b


def _pick_tile(n, pref):
    t = pref
    while n % t:
        t //= 2
    return t


def _gmlp_kernel(x_ref, wuv_ref, vng_ref, vnb_ref, ws_ref, bs_ref, wo_ref, lng_ref, lnb_ref,
                 x1_ref, x1b_ref, vs_ref, y_ref, *, alpha, width, groups, n_prompt_tiles):
    tm = x_ref.shape[0]
    x = x_ref[...]
    h = jnp.dot(x.astype(BF16), wuv_ref[...], preferred_element_type=F32)
    h = 0.5 * h * (1.0 + lax.erf(h * (1.0 / math.sqrt(2.0))))
    u = h[:, :width]
    v = _ln_rows(h[:, width:], vng_ref[...], vnb_ref[...])

    @pl.when(pl.program_id(0) >= n_prompt_tiles)
    def _():
        vs_ref[...] = v

    vb = v.astype(BF16)
    gd = width // groups
    for c in range(tm // CHUNK):
        r0 = c * CHUNK
        for g in range(groups):
            c0 = g * gd
            s = jnp.dot(ws_ref[g], vb[r0:r0 + CHUNK, c0:c0 + gd], preferred_element_type=F32)
            s = s + bs_ref[:, c0:c0 + gd]
            y_ref[r0:r0 + CHUNK, c0:c0 + gd] = (u[r0:r0 + CHUNK, c0:c0 + gd] * s).astype(BF16)
    m = jnp.dot(y_ref[...], wo_ref[...], preferred_element_type=F32)
    x1 = _ln_rows(alpha * x + m, lng_ref[...], lnb_ref[...])
    x1_ref[...] = x1
    x1b_ref[...] = x1.astype(BF16)


def _gmlp_layer(x, n_prompt, w_uv, vn_g, vn_b, ws2, bs2, w_o, ln_g, ln_b, alpha, groups):
    t, d = x.shape
    width = w_o.shape[0]
    tm = 2 * CHUNK
    npt = n_prompt // tm
    nst = (t - n_prompt) // tm
    row = lambda i: (i, 0)
    const2 = lambda i: (0, 0)
    variant = lambda i: (jnp.where(i >= npt, 1, 0), 0, 0, 0)
    variant3 = lambda i: (jnp.where(i >= npt, 1, 0), 0, 0)
    kern = functools.partial(_gmlp_kernel, alpha=alpha, width=width, groups=groups, n_prompt_tiles=npt)
    return pl.pallas_call(
        kern,
        grid=(npt + nst,),
        in_specs=[
            pl.BlockSpec((tm, d), row),
            pl.BlockSpec((d, 2 * width), const2),
            pl.BlockSpec((1, width), const2),
            pl.BlockSpec((1, width), const2),
            pl.BlockSpec((None, groups, CHUNK, CHUNK), variant),
            pl.BlockSpec((None, CHUNK, width), variant3),
            pl.BlockSpec((width, d), const2),
            pl.BlockSpec((1, d), const2),
            pl.BlockSpec((1, d), const2),
        ],
        out_specs=[
            pl.BlockSpec((tm, d), row),
            pl.BlockSpec((tm, d), row),
            pl.BlockSpec((tm, width), lambda i: (jnp.maximum(i - npt, 0), 0)),
        ],
        out_shape=[
            jax.ShapeDtypeStruct((t, d), F32),
            jax.ShapeDtypeStruct((t, d), BF16),
            jax.ShapeDtypeStruct((t - n_prompt, width), F32),
        ],
        scratch_shapes=[pltpu.VMEM((tm, width), BF16)],
        compiler_params=_params("arbitrary"),
        name="gmlp_layer",
    )(x, w_uv, vn_g, vn_b, ws2, bs2, w_o, ln_g, ln_b)


def _swiglu_up_kernel(te_ref, ts_ref, nu_ref, x_ref, wg_ref, wu_ref, h_ref):
    @pl.when(pl.program_id(1) < nu_ref[0])
    def _():
        x = x_ref[...]
        g = jnp.dot(x, wg_ref[...], preferred_element_type=F32)
        u = jnp.dot(x, wu_ref[...], preferred_element_type=F32)
        h_ref[...] = (g * (1.0 / (1.0 + jnp.exp(-g))) * u).astype(BF16)


def _swiglu_up(xb, w_gu, tile_expert, tile_src, n_used, tm):
    d = xb.shape[1]
    f = w_gu.shape[2] // 2
    tf = f // 2
    nf = f // tf
    n_tiles = tile_expert.shape[0]
    return pl.pallas_call(
        _swiglu_up_kernel,
        grid_spec=pltpu.PrefetchScalarGridSpec(
            num_scalar_prefetch=3,
            grid=(nf, n_tiles),
            in_specs=[
                pl.BlockSpec((tm, d), lambda j, t, te, ts, nu: (ts[t], 0)),
                pl.BlockSpec((None, d, tf), lambda j, t, te, ts, nu: (te[t], 0, j)),
                pl.BlockSpec((None, d, tf), lambda j, t, te, ts, nu: (te[t], 0, nf + j)),
            ],
            out_specs=pl.BlockSpec((tm, tf), lambda j, t, te, ts, nu: (jnp.minimum(t, nu[0] - 1), j)),
        ),
        out_shape=jax.ShapeDtypeStruct((n_tiles * tm, f), BF16),
        compiler_params=_params("arbitrary", "arbitrary"),
        name="swiglu_up",
    )(tile_expert, tile_src, n_used, xb, w_gu, w_gu)


def _down_ln_kernel(h_ref, wd_ref, xres_ref, lng_ref, lnb_ref, x2_ref, x2b_ref, *, alpha):
    m = jnp.dot(h_ref[...], wd_ref[...], preferred_element_type=F32)
    x2 = _ln_rows(alpha * xres_ref[...] + m, lng_ref[...], lnb_ref[...])
    x2_ref[...] = x2
    x2b_ref[...] = x2.astype(BF16)


def _down_ln(hb, w_down, xres, ln_g, ln_b, alpha, tm):
    t, f = hb.shape
    d = w_down.shape[1]
    row = lambda i: (i, 0)
    const2 = lambda i: (0, 0)
    return pl.pallas_call(
        functools.partial(_down_ln_kernel, alpha=alpha),
        grid=(t // tm,),
        in_specs=[
            pl.BlockSpec((tm, f), row),
            pl.BlockSpec((f, d), const2),
            pl.BlockSpec((tm, d), row),
            pl.BlockSpec((1, d), const2),
            pl.BlockSpec((1, d), const2),
        ],
        out_specs=[pl.BlockSpec((tm, d), row), pl.BlockSpec((tm, d), row)],
        out_shape=[jax.ShapeDtypeStruct((t, d), F32), jax.ShapeDtypeStruct((t, d), BF16)],
        compiler_params=_params("arbitrary"),
        name="down_ln",
    )(hb, w_down, xres, ln_g, ln_b)


def _down_gate_kernel(te_ref, nu_ref, h_ref, wd_ref, gate_ref, y_ref):
    @pl.when(pl.program_id(0) < nu_ref[0])
    def _():
        m = jnp.dot(h_ref[...], wd_ref[...], preferred_element_type=F32)
        y_ref[...] = (m * gate_ref[:, 0:1]).astype(BF16)


def _down_gate(hg, w_down, gate_rows, tile_expert, n_used, tm):
    f, d = w_down.shape[1], w_down.shape[2]
    n_tiles = tile_expert.shape[0]
    clamp = lambda t, te, nu: (jnp.minimum(t, nu[0] - 1), 0)
    return pl.pallas_call(
        _down_gate_kernel,
        grid_spec=pltpu.PrefetchScalarGridSpec(
            num_scalar_prefetch=2,
            grid=(n_tiles,),
            in_specs=[
                pl.BlockSpec((tm, f), clamp),
                pl.BlockSpec((None, f, d), lambda t, te, nu: (te[t], 0, 0)),
                pl.BlockSpec((tm, LANES), clamp),
            ],
            out_specs=pl.BlockSpec((tm, d), clamp),
        ),
        out_shape=jax.ShapeDtypeStruct((n_tiles * tm, d), BF16),
        compiler_params=_params("arbitrary"),
        name="down_gate",
    )(tile_expert, n_used, hg, w_down, gate_rows)


def _split_bf16(a):
    hi = a.astype(BF16)
    lo = (a - hi.astype(F32)).astype(BF16)
    return hi, lo


def _router_kernel(x_ref, wr_ref, tri_ref, info_ref, base_ref, cnt_ref, *, n_experts):
    i = pl.program_id(0)

    @pl.when(i == 0)
    def _():
        cnt_ref[...] = jnp.zeros_like(cnt_ref)

    xh, xl = _split_bf16(x_ref[...])
    wh, wl = _split_bf16(wr_ref[...])
    logits = (jnp.dot(xh, wh, preferred_element_type=F32) + jnp.dot(xl, wh, preferred_element_type=F32)
              + jnp.dot(xh, wl, preferred_element_type=F32))
    tm = logits.shape[0]
    lane = lax.broadcasted_iota(jnp.int32, (tm, LANES), 1).astype(F32)
    neg = jnp.float32(-jnp.inf)
    lg = jnp.where(lane < n_experts, logits, neg)
    m1 = jnp.max(lg, axis=1, keepdims=True)
    i1 = jnp.min(jnp.where(lg == m1, lane, float(LANES)), axis=1, keepdims=True)
    lg2 = jnp.where(lane == i1, neg, lg)
    m2 = jnp.max(lg2, axis=1, keepdims=True)
    i2 = jnp.min(jnp.where(lg2 == m2, lane, float(LANES)), axis=1, keepdims=True)
    dd = jnp.exp(m2 - m1)
    g1 = 1.0 / (1.0 + dd)
    g2 = dd / (1.0 + dd)
    oh1 = lane == i1
    oh2 = lane == i2
    both = jnp.where(oh1 | oh2, 1.0, 0.0)
    prefix = jnp.dot(tri_ref[...], both.astype(BF16), preferred_element_type=F32)
    base = cnt_ref[0:1, :]
    rank_all = prefix + base
    r1 = jnp.sum(jnp.where(oh1, rank_all, 0.0), axis=1, keepdims=True)
    r2 = jnp.sum(jnp.where(oh2, rank_all, 0.0), axis=1, keepdims=True)
    info = jnp.where(lane == 0, i1, 0.0)
    info = jnp.where(lane == 1, i2, info)
    info = jnp.where(lane == 2, g1, info)
    info = jnp.where(lane == 3, g2, info)
    info = jnp.where(lane == 4, r1, info)
    info = jnp.where(lane == 5, r2, info)
    info_ref[...] = info
    base_ref[...] = base
    cnt_ref[0:1, :] = base + jnp.sum(both, axis=0, keepdims=True)


def _router(x1, w_router, tc):
    t, d = x1.shape
    n_experts = w_router.shape[1]
    wr = jnp.pad(w_router, ((0, 0), (0, LANES - n_experts)))
    tri = (jnp.arange(tc)[:, None] > jnp.arange(tc)[None, :]).astype(BF16)
    n_chunks = t // tc
    info, base, cnt = pl.pallas_call(
        functools.partial(_router_kernel, n_experts=n_experts),
        grid=(n_chunks,),
        in_specs=[
            pl.BlockSpec((tc, d), lambda i: (i, 0)),
            pl.BlockSpec((d, LANES), lambda i: (0, 0)),
            pl.BlockSpec((tc, tc), lambda i: (0, 0)),
        ],
        out_specs=[
            pl.BlockSpec((tc, LANES), lambda i: (i, 0)),
            pl.BlockSpec((None, 1, LANES), lambda i: (i, 0, 0)),
            pl.BlockSpec((8, LANES), lambda i: (0, 0)),
        ],
        out_shape=[
            jax.ShapeDtypeStruct((t, LANES), F32),
            jax.ShapeDtypeStruct((n_chunks, 1, LANES), F32),
            jax.ShapeDtypeStruct((8, LANES), F32),
        ],
        compiler_params=_params("arbitrary"),
        name="router",
    )(x1, wr, tri)
    return info, base[:, 0, :n_experts].astype(jnp.int32), cnt[0, :n_experts].astype(jnp.int32)


def _dispatch_kernel(vt_ref, vc_ref, vf_ref, vok_ref, x_ref, p1_ref, p2_ref, g1_ref, g2_ref,
                     xg_ref, gg_ref):
    v = pl.program_id(0)
    tmg = xg_ref.shape[0]

    @pl.when(vf_ref[v] == 1)
    def _():
        xg_ref[...] = jnp.zeros_like(xg_ref)
        gg_ref[...] = jnp.zeros_like(gg_ref)

    @pl.when(vok_ref[v] == 1)
    def _():
        tc = x_ref.shape[0]
        rows = vt_ref[v] * tmg + lax.broadcasted_iota(jnp.int32, (tmg, tc), 0)
        sel1 = p1_ref[...] == rows
        sel2 = p2_ref[...] == rows
        onehot = jnp.where(sel1 | sel2, 1.0, 0.0).astype(BF16)
        xg_ref[...] += jnp.dot(onehot, x_ref[...], preferred_element_type=F32).astype(BF16)
        gate = jnp.where(sel1, g1_ref[...], 0.0) + jnp.where(sel2, g2_ref[...], 0.0)
        gg_ref[...] += jnp.broadcast_to(jnp.sum(gate, axis=1, keepdims=True), gg_ref.shape)


def _dispatch(xb, pos1, pos2, g1, g2, visits, n_tiles, tmg, tc):
    t, d = xb.shape
    n_chunks = t // tc
    vt, vc, vf, vok = visits
    as_rows = lambda a: a.reshape(n_chunks, 1, tc)
    chunk3 = pl.BlockSpec((None, 1, tc), lambda v, vt, vc, vf, vok: (vc[v], 0, 0))
    tile = lambda v, vt, vc, vf, vok: (vt[v], 0)
    return pl.pallas_call(
        _dispatch_kernel,
        grid_spec=pltpu.PrefetchScalarGridSpec(
            num_scalar_prefetch=4,
            grid=(vt.shape[0],),
            in_specs=[pl.BlockSpec((tc, d), lambda v, vt, vc, vf, vok: (vc[v], 0)),
                      chunk3, chunk3, chunk3, chunk3],
            out_specs=[pl.BlockSpec((tmg, d), tile), pl.BlockSpec((tmg, LANES), tile)],
        ),
        out_shape=[jax.ShapeDtypeStruct((n_tiles * tmg, d), BF16),
                   jax.ShapeDtypeStruct((n_tiles * tmg, LANES), F32)],
        compiler_params=_params("arbitrary"),
        name="moe_dispatch",
    )(vt, vc, vf, vok, xb, as_rows(pos1), as_rows(pos2), as_rows(g1), as_rows(g2))


def _combine_kernel(vt_ref, vc_ref, vf_ref, vl_ref, vok_ref, yg_ref, p1_ref, p2_ref, xres_ref,
                    lng_ref, lnb_ref, x2_ref, x2b_ref, acc_ref, *, alpha):
    v = pl.program_id(0)
    tmg = yg_ref.shape[0]

    @pl.when(vf_ref[v] == 1)
    def _():
        acc_ref[...] = jnp.zeros_like(acc_ref)

    @pl.when(vok_ref[v] == 1)
    def _():
        tc = acc_ref.shape[0]
        rows = vt_ref[v] * tmg + lax.broadcasted_iota(jnp.int32, (tc, tmg), 1)
        onehot = jnp.where((p1_ref[...] == rows) | (p2_ref[...] == rows), 1.0, 0.0).astype(BF16)
        acc_ref[...] += jnp.dot(onehot, yg_ref[...], preferred_element_type=F32)

    @pl.when(vl_ref[v] == 1)
    def _():
        x2 = _ln_rows(alpha * xres_ref[...] + acc_ref[...], lng_ref[...], lnb_ref[...])
        x2_ref[...] = x2
        x2b_ref[...] = x2.astype(BF16)


def _combine(yg, pos1, pos2, xres, ln_g, ln_b, visits, alpha, tmg, tc):
    t, d = xres.shape
    vt, vc, vf, vl, vok = visits
    chunk = lambda v, vt, vc, vf, vl, vok: (vc[v], 0)
    const2 = lambda v, vt, vc, vf, vl, vok: (0, 0)
    return pl.pallas_call(
        functools.partial(_combine_kernel, alpha=alpha),
        grid_spec=pltpu.PrefetchScalarGridSpec(
            num_scalar_prefetch=5,
            grid=(vt.shape[0],),
            in_specs=[
                pl.BlockSpec((tmg, d), lambda v, vt, vc, vf, vl, vok: (vt[v], 0)),
                pl.BlockSpec((tc, 1), chunk),
                pl.BlockSpec((tc, 1), chunk),
                pl.BlockSpec((tc, d), chunk),
                pl.BlockSpec((1, d), const2),
                pl.BlockSpec((1, d), const2),
            ],
            out_specs=[pl.BlockSpec((tc, d), chunk), pl.BlockSpec((tc, d), chunk)],
            scratch_shapes=[pltpu.VMEM((tc, d), F32)],
        ),
        out_shape=[jax.ShapeDtypeStruct((t, d), F32), jax.ShapeDtypeStruct((t, d), BF16)],
        compiler_params=_params("arbitrary"),
        name="moe_combine",
    )(vt, vc, vf, vl, vok, yg, pos1.reshape(t, 1), pos2.reshape(t, 1), xres, ln_g, ln_b)


def _expand_visits(lo, hi, active, n_max):
    n = jnp.where(active, hi - lo + 1, 0)
    end = jnp.cumsum(n)
    start = end - n
    total = end[-1]
    v = jnp.arange(n_max, dtype=jnp.int32)
    ok = v < total
    vv = jnp.minimum(v, total - 1)
    owner = jnp.searchsorted(end, vv, side="right").astype(jnp.int32)
    idx = lo[owner] + (vv - start[owner])
    first = ok & (vv == start[owner])
    last = ok & (vv == end[owner] - 1)
    i32 = lambda a: a.astype(jnp.int32)
    return owner, i32(idx), i32(first), i32(last), i32(ok)


def _moe_layer(x1, x1b, w_router, w_gu, w_down, ln_g, ln_b, alpha, tmg, tc):
    t, d = x1.shape
    n_experts = w_router.shape[1]
    n_chunks = t // tc
    info, base, counts = _router(x1, w_router, tc)
    e1 = info[:, 0].astype(jnp.int32)
    e2 = info[:, 1].astype(jnp.int32)
    g1, g2 = info[:, 2], info[:, 3]
    r1 = info[:, 4].astype(jnp.int32)
    r2 = info[:, 5].astype(jnp.int32)

    n_tiles = (2 * t) // tmg + n_experts
    tiles_e = (counts + tmg - 1) // tmg
    tile_end = jnp.cumsum(tiles_e)
    tile_start = tile_end - tiles_e
    n_used = tile_end[-1]
    off = tile_start * tmg
    pos1 = (off[e1] + r1).astype(jnp.int32)
    pos2 = (off[e2] + r2).astype(jnp.int32)
    tid = jnp.arange(n_tiles, dtype=jnp.int32)
    tile_expert = jnp.minimum(jnp.searchsorted(tile_end, jnp.minimum(tid, n_used - 1), side="right"),
                              n_experts - 1).astype(jnp.int32)

    cum_incl = jnp.concatenate([base[1:], counts[None, :]], axis=0)
    k = tid - tile_start[tile_expert]
    lo_rank = k * tmg
    hi_rank = jnp.minimum(lo_rank + tmg, counts[tile_expert]) - 1
    cum_t = cum_incl[:, tile_expert]
    c_lo = jnp.sum(cum_t <= lo_rank[None, :], axis=0).astype(jnp.int32)
    c_hi = jnp.sum(cum_t <= hi_rank[None, :], axis=0).astype(jnp.int32)
    n_visits = n_experts * n_chunks + n_tiles
    d_owner, d_idx, d_first, _, d_ok = _expand_visits(c_lo, c_hi, tid < n_used, n_visits)
    xg, gate_rows = _dispatch(x1b, pos1, pos2, g1, g2, (d_owner, d_idx, d_first, d_ok), n_tiles, tmg, tc)

    n_used_arr = n_used.reshape(1).astype(jnp.int32)
    hg = _swiglu_up(xg, w_gu, tile_expert, jnp.minimum(tid, n_used - 1), n_used_arr, tmg)
    yg = _down_gate(hg, w_down, gate_rows, tile_expert, n_used_arr, tmg)

    cnt_ce = cum_incl - base
    t_lo = (off[None, :] + base) // tmg
    t_hi = (off[None, :] + cum_incl - 1) // tmg
    c_owner, c_idx, c_first, c_last, c_ok = _expand_visits(
        t_lo.reshape(-1).astype(jnp.int32), t_hi.reshape(-1).astype(jnp.int32),
        (cnt_ce > 0).reshape(-1), n_visits)
    chunk_of = c_owner // n_experts
    prev_chunk = jnp.concatenate([jnp.full((1,), -1, jnp.int32), chunk_of[:-1]])
    next_chunk = jnp.concatenate([chunk_of[1:], jnp.full((1,), -1, jnp.int32)])
    next_ok = jnp.concatenate([c_ok[1:], jnp.zeros((1,), jnp.int32)])
    first = (c_ok == 1) & (chunk_of != prev_chunk)
    last = (c_ok == 1) & ((chunk_of != next_chunk) | (next_ok == 0))
    visits = (c_idx, chunk_of.astype(jnp.int32), first.astype(jnp.int32), last.astype(jnp.int32), c_ok)
    return _combine(yg, pos1, pos2, x1, ln_g, ln_b, visits, alpha, tmg, tc)


def _qkv_kernel(x_ref, w_ref, q_ref, k_ref, v_ref, kb_ref, vb_ref, *, d, scale):
    acc = jnp.dot(x_ref[...], w_ref[...], preferred_element_type=F32)
    k = acc[:, d:2 * d]
    v = acc[:, 2 * d:]
    q_ref[...] = (acc[:, :d] * scale).astype(BF16)
    k_ref[...] = k
    v_ref[...] = v
    kb_ref[...] = k.astype(BF16)
    vb_ref[...] = v.astype(BF16)


def _qkv(xb, w_qkv, scale, tm):
    t, d = xb.shape
    row = lambda i: (i, 0)
    spec = pl.BlockSpec((tm, d), row)
    return pl.pallas_call(
        functools.partial(_qkv_kernel, d=d, scale=scale),
        grid=(t // tm,),
        in_specs=[spec, pl.BlockSpec((d, 3 * d), lambda i: (0, 0))],
        out_specs=[spec] * 5,
        out_shape=[jax.ShapeDtypeStruct((t, d), BF16), jax.ShapeDtypeStruct((t, d), F32),
                   jax.ShapeDtypeStruct((t, d), F32), jax.ShapeDtypeStruct((t, d), BF16),
                   jax.ShapeDtypeStruct((t, d), BF16)],
        compiler_params=_params("arbitrary"),
        name="qkv",
    )(xb, w_qkv)


def _sb_block(z, mask, carry, acc, v_blk, u_tri):
    lg = jnp.log(1.0 + jnp.exp(-jnp.abs(z)))
    sp = jnp.maximum(z, 0.0) + lg
    arg = jnp.minimum(z, 0.0) - lg
    if mask is not None:
        sp = jnp.where(mask, sp, 0.0)
    tail = jnp.dot(sp.astype(BF16), u_tri, preferred_element_type=F32)
    a = jnp.exp(arg - tail - carry)
    if mask is not None:
        a = jnp.where(mask, a, 0.0)
    acc = acc + jnp.dot(a.astype(BF16), v_blk, preferred_element_type=F32)
    carry = carry + jnp.sum(sp, axis=1, keepdims=True)
    return carry, acc


def _sb_prompt_kernel(q_ref, k_ref, v_ref, b_ref, u_ref, o_ref, *, tq, dh):
    s = q_ref.shape[0]
    lane = lax.broadcasted_iota(jnp.int32, (1, LANES), 1)
    diag = lax.broadcasted_iota(jnp.int32, (tq, tq), 1) < lax.broadcasted_iota(jnp.int32, (tq, tq), 0)
    u_tri = u_ref[...]
    nt = (((1,), (1,)), ((), ()))

    def q_body(qi, _):
        qoff = pl.multiple_of(qi * tq, tq)
        q2 = q_ref[pl.ds(qoff, tq), :]
        outs = []
        for hh in range(LANES // dh):
            in_head = (lane >= hh * dh) & (lane < (hh + 1) * dh)
            qh = jnp.where(in_head, q2, jnp.zeros_like(q2))
            bh = b_ref[:, hh * dh:hh * dh + 1]

            def blk(kb, mask, carry, acc):
                koff = pl.multiple_of(kb * tq, tq)
                z = lax.dot_general(qh, k_ref[pl.ds(koff, tq), :], nt, preferred_element_type=F32) + bh
                return _sb_block(z, mask, carry, acc, v_ref[pl.ds(koff, tq), :], u_tri)

            carry, acc = blk(qi, diag, jnp.zeros((tq, 1), F32), jnp.zeros((tq, LANES), F32))
            carry, acc = lax.fori_loop(0, qi, lambda i, ca: blk(qi - 1 - i, None, *ca), (carry, acc))
            outs.append(acc)
        o = outs[0]
        for hh in range(1, len(outs)):
            o = jnp.where(lane >= hh * dh, outs[hh], o)
        o_ref[pl.ds(qoff, tq), :] = o.astype(BF16)
        return 0

    lax.fori_loop(0, s // tq, q_body, 0)


def _sb_prompt(q, kb, vb, bias_lanes, n_batch, seq, dh, tq):
    d = q.shape[1]
    u_tri = (jnp.arange(tq)[:, None] > jnp.arange(tq)[None, :]).astype(BF16)
    blk = pl.BlockSpec((seq, LANES), lambda n, hp: (n, hp))
    return pl.pallas_call(
        functools.partial(_sb_prompt_kernel, tq=tq, dh=dh),
        grid=(n_batch, d // LANES),
        in_specs=[blk, blk, blk,
                  pl.BlockSpec((1, LANES), lambda n, hp: (0, hp)),
                  pl.BlockSpec((tq, tq), lambda n, hp: (0, 0))],
        out_specs=blk,
        out_shape=jax.ShapeDtypeStruct((n_batch * seq, d), BF16),
        compiler_params=_params("arbitrary", "arbitrary"),
        name="sb_prompt",
    )(q, kb, vb, bias_lanes, u_tri)


def _sb_sample_kernel(pt_ref, q_ref, kn_ref, vn_ref, b_ref, u_ref, *refs, n_heads, dh, n_groups):
    np_ = PAGES_PER_STEP
    kp_refs, vp_refs = refs[:np_], refs[np_:2 * np_]
    o_ref, carry_ref, acc_ref = refs[2 * np_:]
    pg = pl.program_id(1)
    u_tri = u_ref[...]
    nn = (((1,), (0,)), ((), ()))
    nt = (((1,), (1,)), ((), ()))
    rows = q_ref.shape[1]
    bias_rows = b_ref[...]

    def sweep(k_of, k_dims, v_of, v_dims, mask):
        z = jnp.concatenate([lax.dot_general(q_ref[h], k_of(h), k_dims, preferred_element_type=F32)
                             for h in range(n_heads)], axis=0) + bias_rows
        lg = jnp.log(1.0 + jnp.exp(-jnp.abs(z)))
        sp = jnp.maximum(z, 0.0) + lg
        arg = jnp.minimum(z, 0.0) - lg
        if mask is not None:
            sp = jnp.where(mask, sp, 0.0)
        tail = jnp.dot(sp.astype(BF16), u_tri, preferred_element_type=F32)
        a = jnp.exp(arg - tail - carry_ref[:, 0:1])
        if mask is not None:
            a = jnp.where(mask, a, 0.0)
        for h in range(n_heads):
            a_h = a[h * rows:(h + 1) * rows].astype(BF16)
            acc_ref[h] += lax.dot_general(a_h, v_of(h), v_dims, preferred_element_type=F32)
        carry_ref[...] += jnp.broadcast_to(jnp.sum(sp, axis=1, keepdims=True), carry_ref.shape)

    @pl.when(pg == 0)
    def _():
        carry_ref[...] = jnp.zeros_like(carry_ref)
        acc_ref[...] = jnp.zeros_like(acc_ref)
        shape = (n_heads * rows, PAGE)
        query = lax.broadcasted_iota(jnp.int32, shape, 0) & (rows - 1)
        mask = lax.broadcasted_iota(jnp.int32, shape, 1) < query
        sweep(lambda h: kn_ref[h], nt, lambda h: vn_ref[h], nn, mask)

    for i in range(np_):
        sweep(lambda h: kp_refs[i][h].astype(BF16), nn, lambda h: vp_refs[i][h].astype(BF16), nt, None)

    @pl.when(pg == n_groups - 1)
    def _():
        o_ref[...] = acc_ref[...]


def _sb_sample(qh, kn, vn, bias, cache_k, cache_v, page_table, layer):
    n_seq, n_heads, rows, dh = qh.shape
    assert rows & (rows - 1) == 0
    n_pages = page_table.shape[1]
    n_groups = n_pages // PAGES_PER_STEP
    u_tri = (jnp.arange(PAGE)[:, None] > jnp.arange(PAGE)[None, :]).astype(BF16)
    seq4 = lambda n, pg, pt: (n, 0, 0, 0)

    def page_spec(i):
        def imap(n, pg, pt):
            return (layer, pt[n * n_pages + (n_pages - 1 - (pg * PAGES_PER_STEP + i))], 0, 0, 0)
        return pl.BlockSpec((None, None, n_heads, dh, PAGE), imap)

    page_specs = [page_spec(i) for i in range(PAGES_PER_STEP)]
    return pl.pallas_call(
        functools.partial(_sb_sample_kernel, n_heads=n_heads, dh=dh, n_groups=n_groups),
        grid_spec=pltpu.PrefetchScalarGridSpec(
            num_scalar_prefetch=1,
            grid=(n_seq, n_groups),
            in_specs=[pl.BlockSpec((None, n_heads, rows, dh), seq4),
                      pl.BlockSpec((None, n_heads, PAGE, dh), seq4),
                      pl.BlockSpec((None, n_heads, PAGE, dh), seq4),
                      pl.BlockSpec((n_heads * rows, 1), lambda n, pg, pt: (0, 0)),
                      pl.BlockSpec((PAGE, PAGE), lambda n, pg, pt: (0, 0))] + page_specs + page_specs,
            out_specs=pl.BlockSpec((None, n_heads, rows, dh), seq4),
            scratch_shapes=[pltpu.VMEM((n_heads * rows, LANES), F32), pltpu.VMEM((n_heads, rows, dh), F32)],
        ),
        out_shape=jax.ShapeDtypeStruct((n_seq, n_heads, rows, dh), F32),
        compiler_params=_params("arbitrary", "arbitrary"),
        name="sb_sample",
    )(page_table.reshape(-1), qh, kn, vn, jnp.repeat(bias.astype(F32), rows)[:, None], u_tri,
      *([cache_k] * PAGES_PER_STEP), *([cache_v] * PAGES_PER_STEP))


def _sb_layer(x, xb, n_prompt, n_batch, dec_seq, w_qkv, bias, w_o, cache_k, cache_v, page_table, layer,
              ln_g, ln_b, alpha, n_heads, tm):
    t, d = x.shape
    dh = d // n_heads
    seq = n_prompt // n_batch
    n_seq = (t - n_prompt) // dec_seq
    q, k, v, kb, vb = _qkv(xb, w_qkv, 1.0 / math.sqrt(dh), tm)
    bias_lanes = jnp.repeat(bias.astype(F32), dh)[None, :]
    o_p = _sb_prompt(q, kb, vb, bias_lanes, n_batch, seq, dh, _pick_tile(seq, 256))

    def heads_first(a, pad_to):
        a = a[n_prompt:].reshape(n_seq, dec_seq, n_heads, dh).transpose(0, 2, 1, 3)
        return jnp.pad(a, ((0, 0), (0, 0), (0, pad_to - dec_seq), (0, 0)))

    o_s = _sb_sample(heads_first(q, 8), heads_first(kb, PAGE), heads_first(vb, PAGE), bias,
                     cache_k, cache_v, page_table, layer)
    o_s = o_s[:, :, :dec_seq].transpose(0, 2, 1, 3).reshape(t - n_prompt, d).astype(BF16)
    o_all = jnp.concatenate([o_p, o_s], axis=0)
    x1, x1b = _down_ln(o_all, w_o, x, ln_g, ln_b, alpha, tm)
    return x1, x1b, k, v


def kernel(x_prompt, x_sample, cache_k, cache_v, page_table, ln_g, ln_b, gm_w_uv, gm_vn_g, gm_vn_b, gm_w_s,
           gm_b_s, gm_w_o, sb_w_qkv, sb_bias, sb_w_o, ffn_w_gu, ffn_w_down, moe_w_router, moe_w_gu,
           moe_w_down):
    n_batch, seq, d = x_prompt.shape
    n_seq, dec_seq, _ = x_sample.shape
    depth = ln_g.shape[0]
    n_heads = sb_bias.shape[1]
    dh = d // n_heads
    groups = gm_w_s.shape[1]
    alpha = float((2 * depth) ** 0.25)
    n_prompt = n_batch * seq
    n_sample = n_seq * dec_seq
    t = n_prompt + n_sample
    tm = _pick_tile(math.gcd(n_prompt, n_sample), 512)

    x = jnp.concatenate([x_prompt.reshape(n_prompt, d), x_sample.reshape(n_sample, d)], axis=0)
    xb = None
    vec = lambda a: a[None, :].astype(F32)
    ck = jnp.transpose(cache_k, (0, 1, 3, 4, 2))
    cv = jnp.transpose(cache_v, (0, 1, 3, 4, 2))

    tril = jnp.tril(jnp.ones((CHUNK, CHUNK), F32))
    idx = jnp.arange(CHUNK)
    same_seq = (idx[:, None] // dec_seq) == (idx[None, :] // dec_seq)

    k_list, v_list, gmv_list = [], [], []
    for i in range(depth):
        j = i // 2
        if i % 2 == 0:
            ws_p = gm_w_s[j] * tril
            ws_s = jnp.where(same_seq, (gm_w_s[j] * tril)[:, idx[:, None] % dec_seq, idx[None, :] % dec_seq], 0.0)
            ws2 = jnp.stack([ws_p, ws_s]).astype(BF16)
            gd = gm_w_o.shape[1] // groups
            bs_p = jnp.repeat(gm_b_s[j].T, gd, axis=1)
            bs_s = jnp.repeat(gm_b_s[j][:, idx % dec_seq].T, gd, axis=1)
            bs2 = jnp.stack([bs_p, bs_s])
            x, xb, vs = _gmlp_layer(x, n_prompt, gm_w_uv[j].astype(BF16), vec(gm_vn_g[j]), vec(gm_vn_b[j]),
                                    ws2, bs2, gm_w_o[j].astype(BF16), vec(ln_g[i, 0]), vec(ln_b[i, 0]),
                                    alpha, groups)
            gmv_list.append(vs.reshape(n_seq, dec_seq, -1))
            n_tiles = t // tm
            tid = jnp.arange(n_tiles, dtype=jnp.int32)
            hb = _swiglu_up(xb, ffn_w_gu[j][None].astype(BF16), jnp.zeros_like(tid), tid,
                            jnp.full((1,), n_tiles, jnp.int32), tm)
            x, xb = _down_ln(hb, ffn_w_down[j].astype(BF16), x, vec(ln_g[i, 1]), vec(ln_b[i, 1]), alpha, tm)
        else:
            x, xb, k, v = _sb_layer(x, xb, n_prompt, n_batch, dec_seq, sb_w_qkv[j].astype(BF16), sb_bias[j],
                                    sb_w_o[j].astype(BF16), ck, cv, page_table, j,
                                    vec(ln_g[i, 0]), vec(ln_b[i, 0]), alpha, n_heads, tm)
            k_list.append(k)
            v_list.append(v)
            x, xb = _moe_layer(x, xb, moe_w_router[j], moe_w_gu[j].astype(BF16), moe_w_down[j].astype(BF16),
                               vec(ln_g[i, 1]), vec(ln_b[i, 1]), alpha, tm, tm)

    ks, vs_ = jnp.stack(k_list), jnp.stack(v_list)
    n_kv = ks.shape[0]
    return (x[:n_prompt].reshape(n_batch, seq, d),
            x[n_prompt:].reshape(n_seq, dec_seq, d),
            ks[:, :n_prompt].reshape(n_kv, n_batch, seq, n_heads, dh),
            vs_[:, :n_prompt].reshape(n_kv, n_batch, seq, n_heads, dh),
            ks[:, n_prompt:].reshape(n_kv, n_seq, dec_seq, n_heads, dh),
            vs_[:, n_prompt:].reshape(n_kv, n_seq, dec_seq, n_heads, dh),
            jnp.stack(gmv_list))
```
